```python
import math, functools
import jax, jax.numpy as jnp
from jax import lax
import numpy as np

D_MODEL = 2048
BATCH = 4
SEQ = 4096
DEPTH = 1
DEC_BATCH = 128
DEC_SEQ = 1
PAST_LEN = 16384
PAGE_SIZE = 128

D_MIX = D_MODEL
SSM_WIDTH = D_MIX // 2
SSM_HEAD_DIM = 64
SSM_HEADS = SSM_WIDTH // SSM_HEAD_DIM
SSM_GROUPS = 2
SSM_STATE = 128
SSM_CONV = 4
SSM_CHUNK = 128
CONV_DIM = SSM_WIDTH + 2 * SSM_GROUPS * SSM_STATE
MLA_WIDTH = D_MIX - SSM_WIDTH
V_HEAD_DIM = 128
MLA_HEADS = MLA_WIDTH // V_HEAD_DIM
QK_NOPE = 128
QK_ROPE = 64
Q_LORA = 512
KV_LORA = 512
ROPE_THETA = 10000.0
Q_BLOCK = 128
ATTN_SCALE = (QK_NOPE + QK_ROPE) ** -0.5
D_FF = 5632
FFN_CONV = 3
LN_EPS = 1e-5
RMS_EPS = 1e-6
ALPHA = (2.0 * DEPTH) ** 0.25
BETA = (8.0 * DEPTH) ** -0.25
OFF_Z = SSM_WIDTH
OFF_XBC = OFF_Z + CONV_DIM
OFF_DT = OFF_XBC + SSM_HEADS
OFF_CQ = OFF_DT + Q_LORA
OFF_CKV = OFF_CQ + KV_LORA
D_IN_PROJ = OFF_CKV + QK_ROPE

kernel_name = "hymba_ssd_mla_convffn_deepnorm_step"


def layer_norm(x, g, b):
    xf = x.astype(jnp.float32)
    mu = jnp.mean(xf, axis=-1, keepdims=True)
    var = jnp.mean(jnp.square(xf - mu), axis=-1, keepdims=True)
    return ((xf - mu) * lax.rsqrt(var + LN_EPS) * g.astype(jnp.float32) + b.astype(jnp.float32)).astype(x.dtype)


def rms_norm(x, g):
    xf = x.astype(jnp.float32)
    r = lax.rsqrt(jnp.mean(jnp.square(xf), axis=-1, keepdims=True) + RMS_EPS)
    return (xf * r * g.astype(jnp.float32)).astype(x.dtype)


def group_rms_norm(y, g, groups):
    b_, t, w = y.shape
    yf = y.astype(jnp.float32).reshape(b_, t, groups, w // groups)
    yf = yf * lax.rsqrt(jnp.mean(jnp.square(yf), axis=-1, keepdims=True) + RMS_EPS)
    return (yf.reshape(b_, t, w) * g.astype(jnp.float32)).astype(y.dtype)


def rope_cos_sin(pos):
    inv_freq = ROPE_THETA ** (-jnp.arange(0, QK_ROPE, 2, dtype=jnp.float32) / QK_ROPE)
    ang = pos.astype(jnp.float32)[:, None] * inv_freq[None, :]
    return jnp.cos(ang), jnp.sin(ang)


def apply_rope(x, cos, sin):
    half = x.shape[-1] // 2
    x1 = x[..., :half].astype(jnp.float32)
    x2 = x[..., half:].astype(jnp.float32)
    return jnp.concatenate([x1 * cos - x2 * sin, x2 * cos + x1 * sin], axis=-1).astype(x.dtype)


def causal_dwconv(u, buf, w, b):
    k = w.shape[0]
    t = u.shape[1]
    ext = jnp.concatenate([buf.astype(u.dtype), u], axis=1)
    y = b + ext[:, 0:t] * w[0]
    for i in range(1, k):
        y = y + ext[:, i:i + t] * w[i]
    return y, ext[:, ext.shape[1] - (k - 1):]


def ssd_scan(xh, dt, a, bm, cm, h0):
    f32 = jnp.float32
    b_, t = xh.shape[:2]
    L = SSM_CHUNK if t % SSM_CHUNK == 0 else t
    nc = t // L
    hg = SSM_HEADS // SSM_GROUPS
    x = xh.astype(f32).reshape(b_, nc, L, SSM_GROUPS, hg, SSM_HEAD_DIM)
    dtc = dt.reshape(b_, nc, L, SSM_GROUPS, hg)
    B = bm.astype(f32).reshape(b_, nc, L, SSM_GROUPS, SSM_STATE)
    C = cm.astype(f32).reshape(b_, nc, L, SSM_GROUPS, SSM_STATE)
    acs = jnp.cumsum(dtc * a.reshape(SSM_GROUPS, hg), axis=2)
    xdt = x * dtc[..., None]
    acs_t = jnp.transpose(acs, (0, 1, 3, 4, 2))
    diff = acs_t[..., :, None] - acs_t[..., None, :]
    causal = jnp.tril(jnp.ones((L, L), dtype=bool))
    decay = jnp.exp(jnp.where(causal, diff, -jnp.inf))
    cb = jnp.einsum("bclgn,bcsgn->bcgls", C, B)
    y_diag = jnp.einsum("bcgls,bcghls,bcsghp->bclghp", cb, decay, xdt)
    decay_s = jnp.exp(acs[:, :, -1:] - acs)
    states = jnp.einsum("bclgn,bclgh,bclghp->bcghpn", B, decay_s, xdt)
    chunk_decay = jnp.exp(acs[:, :, -1])

    def step(h, inp):
        st, dec = inp
        return h * dec[..., None, None] + st, h

    h_init = h0.astype(f32).reshape(b_, SSM_GROUPS, hg, SSM_HEAD_DIM, SSM_STATE)
    h_last, h_prev = lax.scan(step, h_init, (jnp.swapaxes(states, 0, 1), jnp.swapaxes(chunk_decay, 0, 1)))
    h_prev = jnp.swapaxes(h_prev, 0, 1)
    y_off = jnp.einsum("bclgn,bcghpn,bclgh->bclghp", C, h_prev, jnp.exp(acs))
    y = (y_diag + y_off).reshape(b_, t, SSM_HEADS, SSM_HEAD_DIM)
    return y.astype(xh.dtype), h_last.reshape(b_, SSM_HEADS, SSM_HEAD_DIM, SSM_STATE)


def mla_scores(q_lat, q_pe, kc, kp):
    s = jnp.einsum("bqhc,bkc->bhqk", q_lat, kc) + jnp.einsum("bqhr,bkr->bhqk", q_pe, kp)
    return s.astype(jnp.float32) * ATTN_SCALE


def attend_prompt(q_lat, q_pe, ckv, kpe):
    b_, t = q_lat.shape[:2]
    nb = t // Q_BLOCK
    kpos = jnp.arange(t)

    def block(i):
        s0 = i * Q_BLOCK
        ql = lax.dynamic_slice_in_dim(q_lat, s0, Q_BLOCK, axis=1)
        qp = lax.dynamic_slice_in_dim(q_pe, s0, Q_BLOCK, axis=1)
        s = mla_scores(ql, qp, ckv, kpe)
        qpos = s0 + jnp.arange(Q_BLOCK)
        s = jnp.where(kpos[None, :] <= qpos[:, None], s, -jnp.inf)
        p = jax.nn.softmax(s, axis=-1)
        return jnp.einsum("bhqk,bkc->bqhc", p, ckv.astype(jnp.float32)).astype(q_lat.dtype)

    o = lax.map(block, jnp.arange(nb))
    return jnp.transpose(o, (1, 0, 2, 3, 4)).reshape(b_, t, MLA_HEADS, KV_LORA)


def attend_sample(q_lat, q_pe, ckv, kpe, cache_ckv, cache_kpe, page_table, layer):
    t = q_lat.shape[1]
    s = mla_scores(q_lat, q_pe, ckv, kpe)
    s = jnp.where(jnp.tril(jnp.ones((t, t), dtype=bool)), s, -jnp.inf)
    m0 = jnp.max(s, axis=-1)
    p0 = jnp.exp(s - m0[..., None])
    l0 = jnp.sum(p0, axis=-1)
    acc0 = jnp.einsum("bhqk,bkc->bhqc", p0, ckv.astype(jnp.float32))

    def step(carry, pages):
        m, l, acc = carry
        kc = cache_ckv[layer, pages]
        kp = cache_kpe[layer, pages]
        sp = mla_scores(q_lat, q_pe, kc, kp)
        m_new = jnp.maximum(m, jnp.max(sp, axis=-1))
        corr = jnp.exp(m - m_new)
        p = jnp.exp(sp - m_new[..., None])
        l = l * corr + jnp.sum(p, axis=-1)
        acc = acc * corr[..., None] + jnp.einsum("bhqk,bkc->bhqc", p, kc.astype(jnp.float32))
        return (m_new, l, acc), None

    (m, l, acc), _ = lax.scan(step, (m0, l0, acc0), page_table.T)
    o = acc / l[..., None]
    return jnp.transpose(o, (0, 2, 1, 3)).astype(q_lat.dtype)


def decoder_layer(x, pos, conv_buf, ssm_h, ffn_buf, attend, w_in, conv_w, conv_b, dt_bias, a_log, d_skip,
                  ssm_norm_w, q_norm_w, kv_norm_w, w_uq, w_uk, w_uv, w_o, ln1_g, ln1_b,
                  w_ffn_in, ffn_conv_w, ffn_conv_b, w_ffn_down, ln2_g, ln2_b):
    b_, t, _ = x.shape
    proj = jnp.einsum("btd,de->bte", x, w_in)
    z, xbc, dt, cq, ckv, kpe = jnp.split(proj, [OFF_Z, OFF_XBC, OFF_DT, OFF_CQ, OFF_CKV], axis=-1)

    xbc, conv_new = causal_dwconv(xbc, conv_buf, conv_w, conv_b)
    xbc = jax.nn.silu(xbc)
    xs, bm, cm = jnp.split(xbc, [SSM_WIDTH, SSM_WIDTH + SSM_GROUPS * SSM_STATE], axis=-1)
    dt = jax.nn.softplus(dt.astype(jnp.float32) + dt_bias.astype(jnp.float32))
    a = -jnp.exp(a_log.astype(jnp.float32))
    xh = xs.reshape(b_, t, SSM_HEADS, SSM_HEAD_DIM)
    y_ssd, ssm_new = ssd_scan(xh, dt, a, bm.reshape(b_, t, SSM_GROUPS, SSM_STATE),
                              cm.reshape(b_, t, SSM_GROUPS, SSM_STATE), ssm_h)
    y_ssd = (y_ssd + d_skip[:, None] * xh).reshape(b_, t, SSM_WIDTH) * jax.nn.silu(z)
    y_ssd = group_rms_norm(y_ssd, ssm_norm_w, SSM_GROUPS)

    cq = rms_norm(cq, q_norm_w)
    q = jnp.einsum("btc,che->bthe", cq, w_uq)
    q_nope, q_pe = q[..., :QK_NOPE], q[..., QK_NOPE:]
    cos, sin = rope_cos_sin(pos)
    q_pe = apply_rope(q_pe, cos[None, :, None, :], sin[None, :, None, :])
    kpe = apply_rope(kpe, cos[None], sin[None])
    ckv = rms_norm(ckv, kv_norm_w)
    q_lat = jnp.einsum("bthd,chd->bthc", q_nope, w_uk)
    o_lat = attend(q_lat, q_pe, ckv, kpe)
    y_mla = jnp.einsum("bthc,chv->bthv", o_lat, w_uv).reshape(b_, t, MLA_WIDTH)

    mix = jnp.einsum("bte,ed->btd", jnp.concatenate([y_ssd, y_mla], axis=-1), w_o)
    h = layer_norm(ALPHA * x + mix, ln1_g, ln1_b)

    gu = jnp.einsum("btd,df->btf", h, w_ffn_in)
    g, u = jnp.split(gu, [D_FF], axis=-1)
    g, ffn_new = causal_dwconv(g, ffn_buf, ffn_conv_w, ffn_conv_b)
    f = jnp.einsum("btf,fd->btd", jax.nn.silu(g) * u, w_ffn_down)
    y = layer_norm(ALPHA * h + f, ln2_g, ln2_b)
    return y, (ckv, kpe, ssm_new.astype(x.dtype), conv_new, ffn_new)


def setup_inputs(seed: int = 0) -> dict:
    key = jax.random.key(seed)
    ks = jax.random.split(key, 32)
    f32 = jnp.float32
    n_pages = PAST_LEN // PAGE_SIZE
    n_pool = (5 * DEC_BATCH * n_pages + 3) // 4
    nrm = lambda k, shape, scale: jax.random.normal(k, shape, f32) * scale
    dt0 = jnp.exp(jax.random.uniform(ks[11], (DEPTH, SSM_HEADS), f32, math.log(1e-3), math.log(1e-1)))
    page_table = jax.random.permutation(ks[4], n_pool)[:DEC_BATCH * n_pages].reshape(DEC_BATCH, n_pages).astype(jnp.int32)
    return {
        "x_prompt": nrm(ks[0], (BATCH, SEQ, D_MODEL), 1.0),
        "x_sample": nrm(ks[1], (DEC_BATCH, DEC_SEQ, D_MODEL), 1.0),
        "cache_ckv": nrm(ks[2], (DEPTH, n_pool, PAGE_SIZE, KV_LORA), 1.0),
        "cache_kpe": nrm(ks[3], (DEPTH, n_pool, PAGE_SIZE, QK_ROPE), 1.0),
        "page_table": page_table,
        "state_ssm": nrm(ks[5], (DEPTH, DEC_BATCH, SSM_HEADS, SSM_HEAD_DIM, SSM_STATE), 0.5),
        "state_conv": nrm(ks[6], (DEPTH, DEC_BATCH, SSM_CONV - 1, CONV_DIM), 1.0),
        "state_ffn_conv": nrm(ks[7], (DEPTH, DEC_BATCH, FFN_CONV - 1, D_FF), 1.0),
        "w_in": nrm(ks[8], (DEPTH, D_MODEL, D_IN_PROJ), D_MODEL ** -0.5),
        "conv_w": nrm(ks[9], (DEPTH, SSM_CONV, CONV_DIM), SSM_CONV ** -0.5),
        "conv_b": nrm(ks[10], (DEPTH, CONV_DIM), 0.01),
        "dt_bias": dt0 + jnp.log(-jnp.expm1(-dt0)),
        "a_log": jnp.log(jax.random.uniform(ks[12], (DEPTH, SSM_HEADS), f32, 1.0, 16.0)),
        "d_skip": 1.0 + nrm(ks[13], (DEPTH, SSM_HEADS), 0.1),
        "ssm_norm_w": 1.0 + nrm(ks[14], (DEPTH, SSM_WIDTH), 0.02),
        "q_norm_w": 1.0 + nrm(ks[15], (DEPTH, Q_LORA), 0.02),
        "kv_norm_w": 1.0 + nrm(ks[16], (DEPTH, KV_LORA), 0.02),
        "w_uq": nrm(ks[17], (DEPTH, Q_LORA, MLA_HEADS, QK_NOPE + QK_ROPE), Q_LORA ** -0.5),
        "w_uk": nrm(ks[18], (DEPTH, KV_LORA, MLA_HEADS, QK_NOPE), KV_LORA ** -0.5),
        "w_uv": nrm(ks[19], (DEPTH, KV_LORA, MLA_HEADS, V_HEAD_DIM), KV_LORA ** -0.5),
        "w_o": nrm(ks[20], (DEPTH, D_MIX, D_MODEL), BETA * D_MIX ** -0.5),
        "ln1_g": 1.0 + nrm(ks[21], (DEPTH, D_MODEL), 0.02),
        "ln1_b": nrm(ks[22], (DEPTH, D_MODEL), 0.01),
        "w_ffn_in": nrm(ks[23], (DEPTH, D_MODEL, 2 * D_FF), D_MODEL ** -0.5),
        "ffn_conv_w": nrm(ks[24], (DEPTH, FFN_CONV, D_FF), FFN_CONV ** -0.5),
        "ffn_conv_b": nrm(ks[25], (DEPTH, D_FF), 0.01),
        "w_ffn_down": nrm(ks[26], (DEPTH, D_FF, D_MODEL), BETA * D_FF ** -0.5),
        "ln2_g": 1.0 + nrm(ks[27], (DEPTH, D_MODEL), 0.02),
        "ln2_b": nrm(ks[28], (DEPTH, D_MODEL), 0.01),
    }


def reference(x_prompt, x_sample, cache_ckv, cache_kpe, page_table, state_ssm, state_conv, state_ffn_conv,
              w_in, conv_w, conv_b, dt_bias, a_log, d_skip, ssm_norm_w, q_norm_w, kv_norm_w,
              w_uq, w_uk, w_uv, w_o, ln1_g, ln1_b, w_ffn_in, ffn_conv_w, ffn_conv_b, w_ffn_down, ln2_g, ln2_b):
    b_p, t_p = x_prompt.shape[:2]
    past_len = page_table.shape[1] * cache_ckv.shape[2]
    pos_p = jnp.arange(t_p)
    pos_s = past_len + jnp.arange(x_sample.shape[1])
    yp, ys = x_prompt, x_sample
    p_states, s_states = [], []
    for l in range(DEPTH):
        lw = (w_in[l], conv_w[l], conv_b[l], dt_bias[l], a_log[l], d_skip[l], ssm_norm_w[l], q_norm_w[l],
              kv_norm_w[l], w_uq[l], w_uk[l], w_uv[l], w_o[l], ln1_g[l], ln1_b[l], w_ffn_in[l],
              ffn_conv_w[l], ffn_conv_b[l], w_ffn_down[l], ln2_g[l], ln2_b[l])
        conv0 = jnp.zeros((b_p, SSM_CONV - 1, CONV_DIM), x_prompt.dtype)
        ssm0 = jnp.zeros((b_p, SSM_HEADS, SSM_HEAD_DIM, SSM_STATE), jnp.float32)
        ffn0 = jnp.zeros((b_p, FFN_CONV - 1, D_FF), x_prompt.dtype)
        yp, st_p = decoder_layer(yp, pos_p, conv0, ssm0, ffn0, attend_prompt, *lw)
        attend_s = functools.partial(attend_sample, cache_ckv=cache_ckv, cache_kpe=cache_kpe,
                                     page_table=page_table, layer=l)
        ys, st_s = decoder_layer(ys, pos_s, state_conv[l], state_ssm[l], state_ffn_conv[l], attend_s, *lw)
        p_states.append(st_p)
        s_states.append(st_s)
    p_ckv, p_kpe, p_ssm, p_conv, p_ffn_conv = (jnp.stack(a) for a in zip(*p_states))
    s_ckv, s_kpe, s_ssm, s_conv, s_ffn_conv = (jnp.stack(a) for a in zip(*s_states))
    return (yp, ys, p_ckv, p_kpe, p_ssm, p_conv, p_ffn_conv, s_ckv, s_kpe, s_ssm, s_conv, s_ffn_conv)
```

```python
import functools

import jax
import jax.numpy as jnp
from jax import lax
from jax.experimental import pallas as pl
from jax.experimental.pallas import tpu as pltpu

F32 = jnp.float32
BF16 = jnp.bfloat16

D_MODEL = 2048
SSM_WIDTH = 1024
SSM_HEAD_DIM = 64
SSM_HEADS = 16
SSM_GROUPS = 2
SSM_STATE = 128
SSM_CONV = 4
SSM_CHUNK = 128
CONV_DIM = SSM_WIDTH + 2 * SSM_GROUPS * SSM_STATE
GROUP_WIDTH = SSM_WIDTH // SSM_GROUPS
MLA_HEADS = 8
V_HEAD_DIM = 128
QK_NOPE = 128
QK_ROPE = 64
Q_LORA = 512
KV_LORA = 512
ROPE_THETA = 10000.0
ATTN_SCALE = (QK_NOPE + QK_ROPE) ** -0.5
D_FF = 5632
FFN_CONV = 3
LN_EPS = 1e-5
RMS_EPS = 1e-6
DEPTH = 1
ALPHA = (2.0 * DEPTH) ** 0.25
OFF_Z = SSM_WIDTH
OFF_XBC = OFF_Z + CONV_DIM
OFF_DT = OFF_XBC + SSM_HEADS
OFF_CQ = OFF_DT + Q_LORA
OFF_CKV = OFF_CQ + KV_LORA

LANES = 128
SUBLANES = 8
QK_PAD = 2 * LANES
HEAD_PAD = 16
NEG_BIG = -1e30
VMEM_LIMIT = 56 * 1024 * 1024
ROW_TILE = 512
FF_TILE = 512


def _cparams(*sem):
    return pltpu.CompilerParams(dimension_semantics=sem, vmem_limit_bytes=VMEM_LIMIT)


def _dot(a, b):
    return jnp.dot(a, b, preferred_element_type=F32)


def _dot_nt(a, b):
    return lax.dot_general(a, b, (((1,), (1,)), ((), ())), preferred_element_type=F32)


def _split3(x):
    hi = x.astype(BF16)
    r = x - hi.astype(F32)
    mid = r.astype(BF16)
    lo = (r - mid.astype(F32)).astype(BF16)
    return hi, mid, lo


def _dot_exact_lhs(x, sel):
    hi, mid, lo = _split3(x)
    return _dot(hi, sel) + _dot(mid, sel) + _dot(lo, sel)


def _dot_exact_rhs(sel, x):
    hi, mid, lo = _split3(x)
    return _dot(sel, hi) + _dot(sel, mid) + _dot(sel, lo)


def _silu(x):
    return x / (1.0 + jnp.exp(-x))


def _softplus(x):
    return jnp.maximum(x, 0.0) + jnp.log1p(jnp.exp(-jnp.abs(x)))


def _rms(x, g):
    r = lax.rsqrt(jnp.mean(x * x, axis=-1, keepdims=True) + RMS_EPS)
    return x * r * g


def _layer_norm(v, g, b):
    mu = jnp.mean(v, axis=-1, keepdims=True)
    d = v - mu
    var = jnp.mean(d * d, axis=-1, keepdims=True)
    return d * lax.rsqrt(var + LN_EPS) * g + b


def _rope_fold(t):
    return t + pltpu.roll(t, QK_ROPE, 1)


def _resident(shape):
    nd = len(shape)
    return pl.BlockSpec(shape, lambda *_: (0,) * nd)


C_Z = 0
C_XBC = C_Z + SSM_WIDTH
C_CQ = C_XBC + CONV_DIM
C_CKV = C_CQ + Q_LORA
C_KPE = C_CKV + KV_LORA
C_DT = C_KPE + LANES
C_END = C_DT + LANES


def _in_proj_kernel(x_ref, w_ref, tab_ref, dtb_ref, qg_ref, kg_ref,
                    z_ref, xbc_ref, cq_ref, ckv_ref, kpe_ref, kpad_ref, dt_ref):
    xb = x_ref[...].astype(BF16)
    z_ref[...] = _dot(xb, w_ref[:, C_Z:C_XBC])
    xbc_ref[...] = _dot(xb, w_ref[:, C_XBC:C_CQ])
    cq_ref[...] = _rms(_dot(xb, w_ref[:, C_CQ:C_CKV]), qg_ref[...]).astype(BF16)
    ckv_ref[...] = _rms(_dot(xb, w_ref[:, C_CKV:C_KPE]), kg_ref[...])
    kr = _rope_fold(_dot(xb, w_ref[:, C_KPE:C_DT]) * tab_ref[...])
    kpe_ref[...] = kr[:, :QK_ROPE]
    lane = lax.broadcasted_iota(jnp.int32, kr.shape, 1)
    kpad_ref[...] = jnp.where(lane < QK_ROPE, kr, 0.0).astype(BF16)
    dt_ref[...] = _softplus(_dot(xb, w_ref[:, C_DT:C_END]) + dtb_ref[...])


def _in_proj(x, w1, tab, dtb, qg, kg, tm):
    n = x.shape[0]
    n_tab = tab.shape[0] // tm
    row = lambda w: pl.BlockSpec((tm, w), lambda i: (i, 0))
    return pl.pallas_call(
        _in_proj_kernel,
        grid=(n // tm,),
        in_specs=[row(D_MODEL), _resident(w1.shape),
                  pl.BlockSpec((tm, LANES), lambda i: (i % n_tab, 0)),
                  _resident(dtb.shape), _resident(qg.shape), _resident(kg.shape)],
        out_specs=[row(SSM_WIDTH), row(CONV_DIM), row(Q_LORA), row(KV_LORA), row(QK_ROPE), row(LANES),
                   row(LANES)],
        out_shape=[jax.ShapeDtypeStruct((n, SSM_WIDTH), F32), jax.ShapeDtypeStruct((n, CONV_DIM), F32),
                   jax.ShapeDtypeStruct((n, Q_LORA), BF16), jax.ShapeDtypeStruct((n, KV_LORA), F32),
                   jax.ShapeDtypeStruct((n, QK_ROPE), F32), jax.ShapeDtypeStruct((n, LANES), BF16),
                   jax.ShapeDtypeStruct((n, LANES), F32)],
        compiler_params=_cparams("arbitrary"),
        name="in_proj",
    )(x, w1, tab, dtb, qg, kg)


def _gate_and_norm(y, z, nw):
    yg = y * _silu(z)
    parts = []
    for g in range(SSM_GROUPS):
        v = yg[:, g * GROUP_WIDTH:(g + 1) * GROUP_WIDTH]
        parts.append(v * lax.rsqrt(jnp.mean(v * v, axis=-1, keepdims=True) + RMS_EPS))
    return jnp.concatenate(parts, axis=1) * nw


def _ssd_kernel(xbc_ref, z_ref, dt_ref, cw_ref, cb_ref, alog_ref, dsk_ref, nw_ref, e_ref, tril_ref,
                y_ref, hout_ref, ext_ref, ht_ref):
    c = pl.program_id(1)
    L = SSM_CHUNK

    @pl.when(c == 0)
    def _():
        ext_ref[0:SUBLANES, :] = jnp.zeros((SUBLANES, CONV_DIM), F32)
        ht_ref[...] = jnp.zeros_like(ht_ref)

    ext_ref[SUBLANES:SUBLANES + L, :] = xbc_ref[...]
    conv = cb_ref[...] + cw_ref[SSM_CONV - 1:SSM_CONV, :] * xbc_ref[...]
    for k in range(1, SSM_CONV):
        conv = conv + cw_ref[SSM_CONV - 1 - k:SSM_CONV - k, :] * ext_ref[pl.ds(SUBLANES - k, L), :]
    ext_ref[0:SUBLANES, :] = ext_ref[L:L + SUBLANES, :]
    xc = _silu(conv)
    xs = xc[:, :SSM_WIDTH]
    bm = xc[:, SSM_WIDTH:SSM_WIDTH + SSM_GROUPS * SSM_STATE]
    cm = xc[:, SSM_WIDTH + SSM_GROUPS * SSM_STATE:]

    dt = dt_ref[...]
    da = dt * (-jnp.exp(alog_ref[...]))
    acs = _dot_exact_rhs(tril_ref[...], da)
    e = e_ref[...]
    acs_x = _dot_exact_lhs(acs, e)
    dt_x = _dot_exact_lhs(dt, e)
    last_x = acs_x[L - 1:L, :]
    xw = xs * dt_x * jnp.exp(last_x - acs_x)
    exp_acs = jnp.exp(acs_x)
    acs_t = acs.T
    dt_t = dt.T

    row = lax.broadcasted_iota(jnp.int32, (L, L), 0)
    col = lax.broadcasted_iota(jnp.int32, (L, L), 1)
    causal = row >= col
    lower_half = col < SSM_HEAD_DIM

    hprev = ht_ref[...].astype(BF16)
    y_parts = []
    heads_per_group = SSM_HEADS // SSM_GROUPS
    for g in range(SSM_GROUPS):
        gs = slice(g * GROUP_WIDTH, (g + 1) * GROUP_WIDTH)
        bg = bm[:, g * SSM_STATE:(g + 1) * SSM_STATE]
        cg = cm[:, g * SSM_STATE:(g + 1) * SSM_STATE].astype(BF16)
        cb = _dot_nt(cg, bg.astype(BF16))
        y_off = _dot(cg, hprev[:, gs]) * exp_acs[:, gs]
        for jj in range(heads_per_group // 2):
            j = g * (heads_per_group // 2) + jj
            xp = xs[:, j * LANES:(j + 1) * LANES]
            yp = None
            for h, xh in ((2 * j, jnp.where(lower_half, xp, 0.0)), (2 * j + 1, jnp.where(lower_half, 0.0, xp))):
                diff = acs[:, h:h + 1] - acs_t[h:h + 1, :]
                dec = jnp.exp(jnp.where(causal, diff, NEG_BIG))
                m = (cb * dec * dt_t[h:h + 1, :]).astype(BF16)
                t = _dot(m, xh.astype(BF16))
                yp = t if yp is None else yp + t
            y_parts.append(yp + y_off[:, jj * LANES:(jj + 1) * LANES])
        ht_ref[:, gs] = ht_ref[:, gs] * jnp.exp(last_x[:, gs]) + _dot(bg.T.astype(BF16), xw[:, gs].astype(BF16))

    y = jnp.concatenate(y_parts, axis=1) + dsk_ref[...] * xs
    y_ref[...] = _gate_and_norm(y, z_ref[...], nw_ref[...]).astype(BF16)

    @pl.when(c == pl.num_programs(1) - 1)
    def _():
        hout_ref[0] = ht_ref[...].T


def _ssd_prompt(xbc, z, dt, cw, cb, alog, dsk, nw, e, tril, b, t):
    nc = t // SSM_CHUNK
    L = SSM_CHUNK
    row = lambda w: pl.BlockSpec((L, w), lambda bi, c: (bi * nc + c, 0))
    return pl.pallas_call(
        _ssd_kernel,
        grid=(b, nc),
        in_specs=[row(CONV_DIM), row(SSM_WIDTH), row(LANES)] +
                 [_resident(a.shape) for a in (cw, cb, alog, dsk, nw, e, tril)],
        out_specs=[row(SSM_WIDTH), pl.BlockSpec((1, SSM_WIDTH, SSM_STATE), lambda bi, c: (bi, 0, 0))],
        out_shape=[jax.ShapeDtypeStruct((b * t, SSM_WIDTH), BF16),
                   jax.ShapeDtypeStruct((b, SSM_WIDTH, SSM_STATE), F32)],
        scratch_shapes=[pltpu.VMEM((L + SUBLANES, CONV_DIM), F32), pltpu.VMEM((SSM_STATE, SSM_WIDTH), F32)],
        compiler_params=_cparams("arbitrary", "arbitrary"),
        name="ssd_prompt",
    )(xbc, z, dt, cw, cb, alog, dsk, nw, e, tril)


def _ssd_step_kernel(xbc_ref, sc_ref, z_ref, dt_ref, cw_ref, cb_ref, alog_ref, dsk_ref, nw_ref, e_ref, st_ref,
                     y_ref, so_ref, xs_s, b_s, ct_s, xdt_s, da_s, yt_s):
    s = pl.program_id(0)
    n_s = pl.num_programs(0)

    @pl.when(s == 0)
    def _():
        conv = cb_ref[...] + cw_ref[SSM_CONV - 1:SSM_CONV, :] * xbc_ref[...]
        for k in range(SSM_CONV - 1):
            conv = conv + cw_ref[k:k + 1, :] * sc_ref[k]
        xc = _silu(conv)
        xs = xc[:, :SSM_WIDTH]
        xs_s[...] = xs
        b_s[...] = xc[:, SSM_WIDTH:SSM_WIDTH + SSM_GROUPS * SSM_STATE]
        ct_s[...] = xc[:, SSM_WIDTH + SSM_GROUPS * SSM_STATE:].T
        dt = dt_ref[...]
        e = e_ref[...]
        xdt_s[...] = (xs * _dot_exact_lhs(dt, e)).T
        da = jnp.exp(dt * (-jnp.exp(alog_ref[...])))
        da_s[...] = _dot_exact_lhs(da, e).T
        yt_s[...] = jnp.zeros_like(yt_s)

    n_samp = xs_s.shape[0]
    row = lax.broadcasted_iota(jnp.int32, (n_samp, SSM_STATE), 0)
    lane = lax.broadcasted_iota(jnp.int32, (SSM_STATE, n_samp), 1)
    pick_row = row == s
    decay = _dot_exact_lhs(da_s[...], pick_row.astype(BF16))
    b_row = b_s[pl.ds(s, 1), :]
    for g in range(SSM_GROUPS):
        gs = slice(g * GROUP_WIDTH, (g + 1) * GROUP_WIDTH)
        ns = slice(g * SSM_STATE, (g + 1) * SSM_STATE)
        eb = jnp.where(pick_row, b_row[:, ns], 0.0)
        eb_hi = eb.astype(BF16)
        eb_lo = (eb - eb_hi.astype(F32)).astype(BF16)
        upd = _dot_exact_lhs(xdt_s[gs, :], eb_hi) + _dot_exact_lhs(xdt_s[gs, :], eb_lo)
        h_new = st_ref[0, gs, :] * decay[gs, :] + upd
        so_ref[0, gs, :] = h_new
        ce = jnp.where(lane == s, ct_s[ns, :], 0.0).astype(BF16)
        yt_s[gs, :] += _dot(h_new.astype(BF16), ce)

    @pl.when(s == n_s - 1)
    def _():
        y = yt_s[...].T + dsk_ref[...] * xs_s[...]
        y_ref[...] = _gate_and_norm(y, z_ref[...], nw_ref[...]).astype(BF16)


def _ssd_step(xbc, sc, z, dt, cw, cb, alog, dsk, nw, e, state):
    n_s = xbc.shape[0]
    st_spec = pl.BlockSpec((1, SSM_WIDTH, SSM_STATE), lambda s: (s, 0, 0))
    return pl.pallas_call(
        _ssd_step_kernel,
        grid=(n_s,),
        in_specs=[_resident(a.shape) for a in (xbc, sc, z, dt, cw, cb, alog, dsk, nw, e)] + [st_spec],
        out_specs=[_resident((n_s, SSM_WIDTH)), st_spec],
        out_shape=[jax.ShapeDtypeStruct((n_s, SSM_WIDTH), BF16),
                   jax.ShapeDtypeStruct((n_s, SSM_WIDTH, SSM_STATE), F32)],
        scratch_shapes=[pltpu.VMEM((n_s, SSM_WIDTH), F32), pltpu.VMEM((n_s, SSM_GROUPS * SSM_STATE), F32),
                        pltpu.VMEM((SSM_GROUPS * SSM_STATE, n_s), F32), pltpu.VMEM((SSM_WIDTH, n_s), F32),
                        pltpu.VMEM((SSM_WIDTH, n_s), F32), pltpu.VMEM((SSM_WIDTH, n_s), F32)],
        compiler_params=_cparams("arbitrary"),
        name="ssd_step",
    )(xbc, sc, z, dt, cw, cb, alog, dsk, nw, e, state)


def _rope_q_head(qh, tab, lane):
    r = _rope_fold(qh[:, QK_NOPE:] * tab)
    return qh[:, :QK_NOPE], jnp.where(lane < QK_ROPE, r, 0.0)


def _qkv_kernel(cq_ref, ckv_ref, kpad_ref, tab_ref, wq_ref, wuk_ref, wuv_ref, q_ref, k_ref, v_ref):
    q = _dot(cq_ref[...], wq_ref[...])
    ckv = ckv_ref[...].astype(BF16)
    kn = _dot(ckv, wuk_ref[...])
    v = _dot(ckv, wuv_ref[...])
    tab = tab_ref[...]
    kpad = kpad_ref[...]
    lane = lax.broadcasted_iota(jnp.int32, tab.shape, 1)
    for h in range(MLA_HEADS):
        nope, pe = _rope_q_head(q[:, h * QK_PAD:(h + 1) * QK_PAD], tab, lane)
        q_ref[0, h] = jnp.concatenate([nope, pe], axis=1).astype(BF16)
        k_ref[0, h] = jnp.concatenate([kn[:, h * QK_NOPE:(h + 1) * QK_NOPE].astype(BF16), kpad], axis=1)
        v_ref[0, h] = v[:, h * V_HEAD_DIM:(h + 1) * V_HEAD_DIM].astype(BF16)


def _qkv_prompt(cq, ckv, kpad, tab, wq, wuk, wuv, b, t, tm):
    nt = t // tm
    row = lambda w: pl.BlockSpec((tm, w), lambda bi, i: (bi * nt + i, 0))
    head = lambda w: pl.BlockSpec((1, MLA_HEADS, tm, w), lambda bi, i: (bi, 0, i, 0))
    return pl.pallas_call(
        _qkv_kernel,
        grid=(b, nt),
        in_specs=[row(Q_LORA), row(KV_LORA), row(LANES), pl.BlockSpec((tm, LANES), lambda bi, i: (i, 0)),
                  _resident(wq.shape), _resident(wuk.shape), _resident(wuv.shape)],
        out_specs=[head(QK_PAD), head(QK_PAD), head(V_HEAD_DIM)],
        out_shape=[jax.ShapeDtypeStruct((b, MLA_HEADS, t, QK_PAD), BF16),
                   jax.ShapeDtypeStruct((b, MLA_HEADS, t, QK_PAD), BF16),
                   jax.ShapeDtypeStruct((b, MLA_HEADS, t, V_HEAD_DIM), BF16)],
        compiler_params=_cparams("arbitrary", "arbitrary"),
        name="qkv_prompt",
    )(cq, ckv, kpad, tab, wq, wuk, wuv)


def _q_step_kernel(cq_ref, tab_ref, wq_ref, wukt_ref, ql_ref, qp_ref):
    q = _dot(cq_ref[...], wq_ref[...])
    tab = tab_ref[...]
    lane = lax.broadcasted_iota(jnp.int32, tab.shape, 1)
    for h in range(MLA_HEADS):
        nope, pe = _rope_q_head(q[:, h * QK_PAD:(h + 1) * QK_PAD], tab, lane)
        ql_ref[h] = _dot(nope.astype(BF16), wukt_ref[h]).astype(BF16)
        qp_ref[h] = pe.astype(BF16)


def _q_step(cq, tab, wq, wukt):
    n_s = cq.shape[0]
    return pl.pallas_call(
        _q_step_kernel,
        grid=(1,),
        in_specs=[_resident(a.shape) for a in (cq, tab, wq, wukt)],
        out_specs=[_resident((MLA_HEADS, n_s, KV_LORA)), _resident((MLA_HEADS, n_s, LANES))],
        out_shape=[jax.ShapeDtypeStruct((MLA_HEADS, n_s, KV_LORA), BF16),
                   jax.ShapeDtypeStruct((MLA_HEADS, n_s, LANES), BF16)],
        compiler_params=_cparams("arbitrary"),
        name="q_step",
    )(cq, tab, wq, wukt)


def _flash_kernel(q_ref, k_ref, v_ref, o_ref, m_s, l_s, acc_s, *, tq):
    qi = pl.program_id(2)
    q = q_ref[0, 0]
    m_s[...] = jnp.full_like(m_s, NEG_BIG)
    l_s[...] = jnp.zeros_like(l_s)
    acc_s[...] = jnp.zeros_like(acc_s)

    def step(j, masked):
        start = pl.multiple_of(j * tq, tq)
        k = k_ref[0, 0, pl.ds(start, tq), :]
        v = v_ref[0, 0, pl.ds(start, tq), :]
        s = _dot_nt(q, k) * ATTN_SCALE
        if masked:
            row = lax.broadcasted_iota(jnp.int32, s.shape, 0)
            col = lax.broadcasted_iota(jnp.int32, s.shape, 1)
            s = jnp.where(row >= col, s, NEG_BIG)
        m_prev = m_s[...]
        m_new = jnp.maximum(m_prev, jnp.max(s, axis=-1, keepdims=True))
        corr = jnp.exp(m_prev - m_new)
        p = jnp.exp(s - m_new)
        l_s[...] = l_s[...] * corr + jnp.sum(p, axis=-1, keepdims=True)
        acc_s[...] = acc_s[...] * corr + _dot(p.astype(BF16), v)
        m_s[...] = m_new

    def body(j, carry):
        step(j, False)
        return carry

    lax.fori_loop(0, qi, body, 0)
    step(qi, True)
    o_ref[0] = (acc_s[...] / l_s[...]).astype(BF16)


def _flash_prompt(q, k, v, tq):
    b, h, t, _ = q.shape
    return pl.pallas_call(
        functools.partial(_flash_kernel, tq=tq),
        grid=(b, h, t // tq),
        in_specs=[pl.BlockSpec((1, 1, tq, QK_PAD), lambda bi, hi, qi: (bi, hi, qi, 0)),
                  pl.BlockSpec((1, 1, t, QK_PAD), lambda bi, hi, qi: (bi, hi, 0, 0)),
                  pl.BlockSpec((1, 1, t, V_HEAD_DIM), lambda bi, hi, qi: (bi, hi, 0, 0))],
        out_specs=pl.BlockSpec((1, tq, V_HEAD_DIM), lambda bi, hi, qi: (bi, qi, hi)),
        out_shape=jax.ShapeDtypeStruct((b, t, h * V_HEAD_DIM), BF16),
        scratch_shapes=[pltpu.VMEM((tq, 1), F32), pltpu.VMEM((tq, 1), F32), pltpu.VMEM((tq, V_HEAD_DIM), F32)],
        compiler_params=_cparams("arbitrary", "arbitrary", "arbitrary"),
        name="flash_prompt",
    )(q, k, v)


def _decode_kernel(pt_ref, ql_ref, qp_ref, cn_ref, kn_ref, *rest, n_pg):
    ckv_pages = rest[:n_pg]
    kpe_pages = rest[n_pg:2 * n_pg]
    o_ref, m_s, l_s, acc_s = rest[2 * n_pg:]
    j = pl.program_id(1)
    ql = ql_ref[0]
    qp = qp_ref[0]

    @pl.when(j == 0)
    def _():
        cn = cn_ref[...].astype(BF16).astype(F32)
        kn = kn_ref[...].astype(BF16).astype(F32)
        s0 = (jnp.sum(ql.astype(F32) * cn, axis=-1, keepdims=True)
              + jnp.sum(qp.astype(F32) * kn, axis=-1, keepdims=True)) * ATTN_SCALE
        m_s[...] = s0
        l_s[...] = jnp.ones_like(l_s)
        acc_s[...] = jnp.broadcast_to(cn, acc_s.shape)

    for i in range(n_pg):
        kc = ckv_pages[i][...].astype(BF16)
        kp = kpe_pages[i][...].astype(BF16)
        s = (_dot_nt(ql, kc) + _dot_nt(qp[:, :QK_ROPE], kp)) * ATTN_SCALE
        m_prev = m_s[...]
        m_new = jnp.maximum(m_prev, jnp.max(s, axis=-1, keepdims=True))
        corr = jnp.exp(m_prev - m_new)
        p = jnp.exp(s - m_new)
        l_s[...] = l_s[...] * corr + jnp.sum(p, axis=-1, keepdims=True)
        acc_s[...] = acc_s[...] * corr + _dot(p.astype(BF16), kc)
        m_s[...] = m_new

    @pl.when(j == pl.num_programs(1) - 1)
    def _():
        o_ref[0] = acc_s[...] / l_s[...]


def _decode_attention(page_table, ql, qp, ckv_new, kpe_new, cache_ckv, cache_kpe, n_pg):
    n_s, n_pages = page_table.shape
    page = cache_ckv.shape[2]
    ckv_specs = [pl.BlockSpec((None, None, page, KV_LORA),
                              functools.partial(lambda s, j, pt, i: (0, pt[s, j * n_pg + i], 0, 0), i=i))
                 for i in range(n_pg)]
    kpe_specs = [pl.BlockSpec((None, None, page, QK_ROPE),
                              functools.partial(lambda s, j, pt, i: (0, pt[s, j * n_pg + i], 0, 0), i=i))
                 for i in range(n_pg)]
    grid_spec = pltpu.PrefetchScalarGridSpec(
        num_scalar_prefetch=1,
        grid=(n_s, n_pages // n_pg),
        in_specs=[pl.BlockSpec((1, HEAD_PAD, KV_LORA), lambda s, j, pt: (s, 0, 0)),
                  pl.BlockSpec((1, HEAD_PAD, LANES), lambda s, j, pt: (s, 0, 0)),
                  pl.BlockSpec((None, 1, KV_LORA), lambda s, j, pt: (s, 0, 0)),
                  pl.BlockSpec((None, 1, LANES), lambda s, j, pt: (s, 0, 0))] + ckv_specs + kpe_specs,
        out_specs=pl.BlockSpec((1, HEAD_PAD, KV_LORA), lambda s, j, pt: (s, 0, 0)),
        scratch_shapes=[pltpu.VMEM((HEAD_PAD, 1), F32), pltpu.VMEM((HEAD_PAD, 1), F32),
                        pltpu.VMEM((HEAD_PAD, KV_LORA), F32)],
    )
    return pl.pallas_call(
        functools.partial(_decode_kernel, n_pg=n_pg),
        grid_spec=grid_spec,
        out_shape=jax.ShapeDtypeStruct((n_s, HEAD_PAD, KV_LORA), F32),
        compiler_params=_cparams("arbitrary", "arbitrary"),
        name="decode_attention",
    )(page_table, ql, qp, ckv_new, kpe_new, *([cache_ckv] * n_pg), *([cache_kpe] * n_pg))


def _uv_kernel(o_ref, wuv_ref, y_ref):
    for h in range(MLA_HEADS):
        y_ref[:, h * V_HEAD_DIM:(h + 1) * V_HEAD_DIM] = _dot(o_ref[h], wuv_ref[h]).astype(BF16)


def _uv_step(o, wuv):
    n_s = o.shape[1]
    return pl.pallas_call(
        _uv_kernel,
        grid=(1,),
        in_specs=[_resident(o.shape), _resident(wuv.shape)],
        out_specs=_resident((n_s, MLA_HEADS * V_HEAD_DIM)),
        out_shape=jax.ShapeDtypeStruct((n_s, MLA_HEADS * V_HEAD_DIM), BF16),
        compiler_params=_cparams("arbitrary"),
        name="uv_step",
    )(o, wuv)


def _wo_ln_kernel(ys_ref, ym_ref, x_ref, wo_ref, g_ref, b_ref, h_ref):
    mix = _dot(ys_ref[...], wo_ref[0:SSM_WIDTH, :]) + _dot(ym_ref[...], wo_ref[SSM_WIDTH:, :])
    h_ref[...] = _layer_norm(ALPHA * x_ref[...] + mix, g_ref[...], b_ref[...])


def _wo_ln(ys, ym, x, wo, g, b, tm):
    n = x.shape[0]
    row = lambda w: pl.BlockSpec((tm, w), lambda i: (i, 0))
    return pl.pallas_call(
        _wo_ln_kernel,
        grid=(n // tm,),
        in_specs=[row(SSM_WIDTH), row(SSM_WIDTH), row(D_MODEL), _resident(wo.shape), _resident(g.shape),
                  _resident(b.shape)],
        out_specs=row(D_MODEL),
        out_shape=jax.ShapeDtypeStruct((n, D_MODEL), F32),
        compiler_params=_cparams("arbitrary"),
        name="wo_ln",
    )(ys, ym, x, wo, g, b)


def _ffn_tail(j, h_ref, act, wd_ref, g2_ref, b2_ref, y_ref, acc_s):
    acc_s[...] += _dot(act.astype(BF16), wd_ref[...])

    @pl.when(j == pl.num_programs(1) - 1)
    def _():
        y_ref[...] = _layer_norm(ALPHA * h_ref[...] + acc_s[...], g2_ref[...], b2_ref[...])


def _ffn_head(j, h_ref, wg_ref, wu_ref, hb_s, acc_s):
    @pl.when(j == 0)
    def _():
        hb_s[...] = h_ref[...].astype(BF16)
        acc_s[...] = jnp.zeros_like(acc_s)

    hb = hb_s[...]
    return _dot(hb, wg_ref[...]), _dot(hb, wu_ref[...])


def _ffn_prompt_kernel(h_ref, wg_ref, wu_ref, wd_ref, cw_ref, cb_ref, g2_ref, b2_ref, y_ref, gl_ref,
                       hb_s, acc_s, ext_s, carry_s, *, blocks_per_seq):
    i = pl.program_id(0)
    j = pl.program_id(1)
    tm = h_ref.shape[0]
    g, u = _ffn_head(j, h_ref, wg_ref, wu_ref, hb_s, acc_s)

    @pl.when(i % blocks_per_seq == 0)
    def _():
        carry_s[j] = jnp.zeros(carry_s.shape[1:], F32)

    ext_s[0:SUBLANES, :] = carry_s[j]
    ext_s[SUBLANES:, :] = g
    gc = cb_ref[...] + cw_ref[FFN_CONV - 1:FFN_CONV, :] * g
    for k in range(1, FFN_CONV):
        gc = gc + cw_ref[FFN_CONV - 1 - k:FFN_CONV - k, :] * ext_s[pl.ds(SUBLANES - k, tm), :]
    tail = ext_s[tm:tm + SUBLANES, :]
    carry_s[j] = tail
    gl_ref[0] = tail
    _ffn_tail(j, h_ref, _silu(gc) * u, wd_ref, g2_ref, b2_ref, y_ref, acc_s)


def _ffn_step_kernel(h_ref, wg_ref, wu_ref, wd_ref, cw_ref, cb_ref, p2_ref, p1_ref, g2_ref, b2_ref,
                     y_ref, gout_ref, hb_s, acc_s):
    j = pl.program_id(1)
    g, u = _ffn_head(j, h_ref, wg_ref, wu_ref, hb_s, acc_s)
    gout_ref[...] = g
    gc = cb_ref[...] + cw_ref[0:1, :] * p2_ref[...] + cw_ref[1:2, :] * p1_ref[...] + cw_ref[2:3, :] * g
    _ffn_tail(j, h_ref, _silu(gc) * u, wd_ref, g2_ref, b2_ref, y_ref, acc_s)


def _ffn_specs(tm, tf):
    nj = D_FF // tf
    return [pl.BlockSpec((tm, D_MODEL), lambda i, j: (i, 0)),
            pl.BlockSpec((D_MODEL, tf), lambda i, j: (0, j)),
            pl.BlockSpec((D_MODEL, tf), lambda i, j: (0, j + nj)),
            pl.BlockSpec((tf, D_MODEL), lambda i, j: (j, 0)),
            pl.BlockSpec((FFN_CONV, tf), lambda i, j: (0, j)),
            pl.BlockSpec((1, tf), lambda i, j: (0, j))]


def _ffn_prompt(h, w_in, w_down, cw, cb, g2, b2, b, t, tm, tf):
    n = h.shape[0]
    nj = D_FF // tf
    bps = t // tm
    vec = pl.BlockSpec((1, D_MODEL), lambda i, j: (0, 0))
    return pl.pallas_call(
        functools.partial(_ffn_prompt_kernel, blocks_per_seq=bps),
        grid=(n // tm, nj),
        in_specs=_ffn_specs(tm, tf) + [vec, vec],
        out_specs=[pl.BlockSpec((tm, D_MODEL), lambda i, j: (i, 0)),
                   pl.BlockSpec((1, SUBLANES, tf), lambda i, j: (i, 0, j))],
        out_shape=[jax.ShapeDtypeStruct((n, D_MODEL), F32), jax.ShapeDtypeStruct((n // tm, SUBLANES, D_FF), F32)],
        scratch_shapes=[pltpu.VMEM((tm, D_MODEL), BF16), pltpu.VMEM((tm, D_MODEL), F32),
                        pltpu.VMEM((tm + SUBLANES, tf), F32), pltpu.VMEM((nj, SUBLANES, tf), F32)],
        compiler_params=_cparams("arbitrary", "arbitrary"),
        name="ffn_prompt",
    )(h, w_in, w_in, w_down, cw, cb, g2, b2)


def _ffn_step(h, w_in, w_down, cw, cb, p2, p1, g2, b2, tf):
    n = h.shape[0]
    nj = D_FF // tf
    vec = pl.BlockSpec((1, D_MODEL), lambda i, j: (0, 0))
    col = pl.BlockSpec((n, tf), lambda i, j: (0, j))
    return pl.pallas_call(
        _ffn_step_kernel,
        grid=(1, nj),
        in_specs=_ffn_specs(n, tf) + [col, col, vec, vec],
        out_specs=[pl.BlockSpec((n, D_MODEL), lambda i, j: (0, 0)), col],
        out_shape=[jax.ShapeDtypeStruct((n, D_MODEL), F32), jax.ShapeDtypeStruct((n, D_FF), F32)],
        scratch_shapes=[pltpu.VMEM((n, D_MODEL), BF16), pltpu.VMEM((n, D_MODEL), F32)],
        compiler_params=_cparams("arbitrary", "arbitrary"),
        name="ffn_step",
    )(h, w_in, w_in, w_down, cw, cb, p2, p1, g2, b2)


def _rope_table(pos):
    inv_freq = ROPE_THETA ** (-jnp.arange(0, QK_ROPE, 2, dtype=F32) / QK_ROPE)
    ang = pos.astype(F32)[:, None] * inv_freq[None, :]
    c, s = jnp.cos(ang), jnp.sin(ang)
    return jnp.concatenate([c, c, -s, s], axis=1)


def _swap_halves(w):
    half = w.shape[-1] // 2
    return jnp.concatenate([w[..., half:], w[..., :half]], axis=-1)


def _pad_lanes(v):
    return jnp.pad(v.reshape(1, -1).astype(F32), ((0, 0), (0, LANES - v.shape[-1])))


def _tile(n, cap):
    t = min(n, cap)
    assert n % t == 0
    return t


def kernel(x_prompt, x_sample, cache_ckv, cache_kpe, page_table, state_ssm, state_conv, state_ffn_conv,
           w_in, conv_w, conv_b, dt_bias, a_log, d_skip, ssm_norm_w, q_norm_w, kv_norm_w,
           w_uq, w_uk, w_uv, w_o, ln1_g, ln1_b, w_ffn_in, ffn_conv_w, ffn_conv_b, w_ffn_down, ln2_g, ln2_b):
    assert w_in.shape[0] == DEPTH == 1 and x_sample.shape[1] == 1
    b, t, _ = x_prompt.shape
    n_s = x_sample.shape[0]
    n_pages = page_table.shape[1]
    past_len = n_pages * cache_ckv.shape[2]
    assert t % SSM_CHUNK == 0

    wi = w_in[0]
    w_kpe = wi[:, OFF_CKV:]
    w1 = jnp.concatenate([wi[:, :OFF_XBC], wi[:, OFF_DT:OFF_CKV], w_kpe, _swap_halves(w_kpe),
                          wi[:, OFF_XBC:OFF_DT], jnp.zeros((D_MODEL, LANES - SSM_HEADS), F32)], axis=1).astype(BF16)
    uq = w_uq[0]
    uq_pe = uq[:, :, QK_NOPE:]
    wq = jnp.concatenate([uq[:, :, :QK_NOPE], uq_pe, _swap_halves(uq_pe)], axis=-1)
    wq = wq.reshape(Q_LORA, MLA_HEADS * QK_PAD).astype(BF16)
    wuk = w_uk[0].reshape(KV_LORA, MLA_HEADS * QK_NOPE).astype(BF16)
    wuv = w_uv[0].reshape(KV_LORA, MLA_HEADS * V_HEAD_DIM).astype(BF16)
    wukt = jnp.transpose(w_uk[0], (1, 2, 0)).astype(BF16)
    wuv_h = jnp.transpose(w_uv[0], (1, 0, 2)).astype(BF16)
    wo = w_o[0].astype(BF16)
    wf_in = w_ffn_in[0].astype(BF16)
    wf_down = w_ffn_down[0].astype(BF16)
    row = lambda v: v.reshape(1, -1).astype(F32)
    dtb, alog = _pad_lanes(dt_bias[0]), _pad_lanes(a_log[0])
    dsk = row(jnp.repeat(d_skip[0], SSM_HEAD_DIM))
    nw, qg, kg = row(ssm_norm_w[0]), row(q_norm_w[0]), row(kv_norm_w[0])
    cw, cb = conv_w[0], row(conv_b[0])
    fcw, fcb = ffn_conv_w[0], row(ffn_conv_b[0])
    g1, b1, g2, b2 = row(ln1_g[0]), row(ln1_b[0]), row(ln2_g[0]), row(ln2_b[0])
    expand = (jnp.arange(SSM_WIDTH)[None, :] // SSM_HEAD_DIM == jnp.arange(LANES)[:, None]).astype(BF16)
    tril = (jnp.arange(SSM_CHUNK)[:, None] >= jnp.arange(SSM_CHUNK)[None, :]).astype(BF16)
    tab_p = _rope_table(jnp.arange(t))
    tab_s = _rope_table(jnp.full((n_s,), past_len))

    tm = _tile(t, ROW_TILE)
    xp = x_prompt.reshape(b * t, D_MODEL)
    z, xbc, cq, ckv, kpe, kpad, dt = _in_proj(xp, w1, tab_p, dtb, qg, kg, tm)
    y_ssd, h_fin = _ssd_prompt(xbc, z, dt, cw, cb, alog, dsk, nw, expand, tril, b, t)
    q, k, v = _qkv_prompt(cq, ckv, kpad, tab_p, wq, wuk, wuv, b, t, tm)
    y_mla = _flash_prompt(q, k, v, tm).reshape(b * t, MLA_HEADS * V_HEAD_DIM)
    h1 = _wo_ln(y_ssd, y_mla, xp, wo, g1, b1, tm)
    tf = FF_TILE
    y_p, g_last = _ffn_prompt(h1, wf_in, wf_down, fcw, fcb, g2, b2, b, t, tm, tf)

    xs_ = x_sample.reshape(n_s, D_MODEL)
    z_s, xbc_s, cq_s, ckv_s, kpe_s, kpad_s, dt_s = _in_proj(xs_, w1, tab_s, dtb, qg, kg, n_s)
    sc = jnp.transpose(state_conv[0], (1, 0, 2))
    y_ssd_s, st_new = _ssd_step(xbc_s, sc, z_s, dt_s, cw, cb, alog, dsk, nw, expand,
                                state_ssm[0].reshape(n_s, SSM_WIDTH, SSM_STATE))
    ql, qp = _q_step(cq_s, tab_s, wq, wukt)
    pad_heads = lambda a: jnp.pad(jnp.transpose(a, (1, 0, 2)), ((0, 0), (0, HEAD_PAD - MLA_HEADS), (0, 0)))
    n_pg = 8 if n_pages % 8 == 0 else 1
    o_lat = _decode_attention(page_table, pad_heads(ql), pad_heads(qp), ckv_s.reshape(n_s, 1, KV_LORA),
                              kpad_s.astype(F32).reshape(n_s, 1, LANES), cache_ckv, cache_kpe, n_pg)
    y_mla_s = _uv_step(jnp.transpose(o_lat[:, :MLA_HEADS], (1, 0, 2)).astype(BF16), wuv_h)
    h1_s = _wo_ln(y_ssd_s, y_mla_s, xs_, wo, g1, b1, n_s)
    fbuf = state_ffn_conv[0]
    y_s, g_s = _ffn_step(h1_s, wf_in, wf_down, fcw, fcb, fbuf[:, 0], fbuf[:, 1], g2, b2, tf)

    lead = lambda a: a[None]
    return (y_p.reshape(b, t, D_MODEL),
            y_s.reshape(n_s, 1, D_MODEL),
            lead(ckv.reshape(b, t, KV_LORA)),
            lead(kpe.reshape(b, t, QK_ROPE)),
            lead(h_fin.reshape(b, SSM_HEADS, SSM_HEAD_DIM, SSM_STATE)),
            lead(xbc.reshape(b, t, CONV_DIM)[:, t - (SSM_CONV - 1):]),
            lead(g_last.reshape(b, t // tm, SUBLANES, D_FF)[:, -1, SUBLANES - (FFN_CONV - 1):]),
            lead(ckv_s.reshape(n_s, 1, KV_LORA)),
            lead(kpe_s.reshape(n_s, 1, QK_ROPE)),
            lead(st_new.reshape(n_s, SSM_HEADS, SSM_HEAD_DIM, SSM_STATE)),
            lead(jnp.concatenate([state_conv[0][:, 1:], xbc_s[:, None]], axis=1)),
            lead(jnp.concatenate([fbuf[:, 1:], g_s[:, None]], axis=1)))
```

```python
import functools
import math

import jax
import jax.numpy as jnp
from jax import lax
from jax.experimental import pallas as pl
from jax.experimental.pallas import tpu as pltpu

F32 = jnp.float32
BF16 = jnp.bfloat16

D_MODEL = 2048
SSM_WIDTH = 1024
SSM_HEAD_DIM = 64
SSM_HEADS = 16
SSM_GROUPS = 2
SSM_STATE = 128
SSM_CONV = 4
SSM_CHUNK = 128
CONV_DIM = SSM_WIDTH + 2 * SSM_GROUPS * SSM_STATE
GROUP_WIDTH = SSM_WIDTH // SSM_GROUPS
MLA_HEADS = 8
V_HEAD_DIM = 128
QK_NOPE = 128
QK_ROPE = 64
Q_LORA = 512
KV_LORA = 512
ROPE_THETA = 10000.0
ATTN_SCALE = (QK_NOPE + QK_ROPE) ** -0.5
D_FF = 5632
FFN_CONV = 3
LN_EPS = 1e-5
RMS_EPS = 1e-6
DEPTH = 1
ALPHA = (2.0 * DEPTH) ** 0.25
OFF_Z = SSM_WIDTH
OFF_XBC = OFF_Z + CONV_DIM
OFF_DT = OFF_XBC + SSM_HEADS
OFF_CQ = OFF_DT + Q_LORA
OFF_CKV = OFF_CQ + KV_LORA

LANES = 128
SUBLANES = 8
QK_PAD = 2 * LANES
HEAD_PAD = 16
NEG_BIG = -1e30
VMEM_LIMIT = 56 * 1024 * 1024
ROW_TILE = 512
FF_TILE = 512
FLASH_HEADS = 4
FLASH_TILE = 512
DECODE_PAGES = 16
EXP2_SCALE = ATTN_SCALE * math.log2(math.e)


def _cparams(*sem):
    return pltpu.CompilerParams(dimension_semantics=sem, vmem_limit_bytes=VMEM_LIMIT)


def _dot(a, b):
    return jnp.dot(a, b, preferred_element_type=F32)


def _dot_nt(a, b):
    return lax.dot_general(a, b, (((1,), (1,)), ((), ())), preferred_element_type=F32)


def _split3(x):
    hi = x.astype(BF16)
    r = x - hi.astype(F32)
    mid = r.astype(BF16)
    lo = (r - mid.astype(F32)).astype(BF16)
    return hi, mid, lo


def _dot_exact_lhs(x, sel):
    hi, mid, lo = _split3(x)
    return _dot(hi, sel) + _dot(mid, sel) + _dot(lo, sel)


def _dot_exact_rhs(sel, x):
    hi, mid, lo = _split3(x)
    return _dot(sel, hi) + _dot(sel, mid) + _dot(sel, lo)


def _silu(x):
    return x / (1.0 + jnp.exp(-x))


def _softplus(x):
    return jnp.maximum(x, 0.0) + jnp.log1p(jnp.exp(-jnp.abs(x)))


def _rms(x, g):
    r = lax.rsqrt(jnp.mean(x * x, axis=-1, keepdims=True) + RMS_EPS)
    return x * r * g


def _layer_norm(v, g, b):
    mu = jnp.mean(v, axis=-1, keepdims=True)
    d = v - mu
    var = jnp.mean(d * d, axis=-1, keepdims=True)
    return d * lax.rsqrt(var + LN_EPS) * g + b


def _rope_fold(t):
    return t + pltpu.roll(t, QK_ROPE, 1)


def _resident(shape):
    nd = len(shape)
    return pl.BlockSpec(shape, lambda *_: (0,) * nd)


C_Z = 0
C_XBC = C_Z + SSM_WIDTH
C_CQ = C_XBC + CONV_DIM
C_CKV = C_CQ + Q_LORA
C_KPE = C_CKV + KV_LORA
C_DT = C_KPE + LANES
C_END = C_DT + LANES


def _in_proj_kernel(x_ref, w_ref, tab_ref, dtb_ref, qg_ref, kg_ref,
                    z_ref, xbc_ref, cq_ref, ckv_ref, kpe_ref, kpad_ref, dt_ref):
    xb = x_ref[...].astype(BF16)
    z_ref[...] = _dot(xb, w_ref[:, C_Z:C_XBC])
    xbc_ref[...] = _dot(xb, w_ref[:, C_XBC:C_CQ])
    cq_ref[...] = _rms(_dot(xb, w_ref[:, C_CQ:C_CKV]), qg_ref[...]).astype(BF16)
    ckv_ref[...] = _rms(_dot(xb, w_ref[:, C_CKV:C_KPE]), kg_ref[...])
    kr = _rope_fold(_dot(xb, w_ref[:, C_KPE:C_DT]) * tab_ref[...])
    kpe_ref[...] = kr[:, :QK_ROPE]
    lane = lax.broadcasted_iota(jnp.int32, kr.shape, 1)
    kpad_ref[...] = jnp.where(lane < QK_ROPE, kr, 0.0).astype(BF16)
    dt_ref[...] = _softplus(_dot(xb, w_ref[:, C_DT:C_END]) + dtb_ref[...])


def _in_proj(x, w1, tab, dtb, qg, kg, tm):
    n = x.shape[0]
    n_tab = tab.shape[0] // tm
    row = lambda w: pl.BlockSpec((tm, w), lambda i: (i, 0))
    return pl.pallas_call(
        _in_proj_kernel,
        grid=(n // tm,),
        in_specs=[row(D_MODEL), _resident(w1.shape),
                  pl.BlockSpec((tm, LANES), lambda i: (i % n_tab, 0)),
                  _resident(dtb.shape), _resident(qg.shape), _resident(kg.shape)],
        out_specs=[row(SSM_WIDTH), row(CONV_DIM), row(Q_LORA), row(KV_LORA), row(QK_ROPE), row(LANES),
                   row(LANES)],
        out_shape=[jax.ShapeDtypeStruct((n, SSM_WIDTH), F32), jax.ShapeDtypeStruct((n, CONV_DIM), F32),
                   jax.ShapeDtypeStruct((n, Q_LORA), BF16), jax.ShapeDtypeStruct((n, KV_LORA), F32),
                   jax.ShapeDtypeStruct((n, QK_ROPE), F32), jax.ShapeDtypeStruct((n, LANES), BF16),
                   jax.ShapeDtypeStruct((n, LANES), F32)],
        compiler_params=_cparams("arbitrary"),
        name="in_proj",
    )(x, w1, tab, dtb, qg, kg)


def _gate_and_norm(y, z, nw):
    yg = y * _silu(z)
    parts = []
    for g in range(SSM_GROUPS):
        v = yg[:, g * GROUP_WIDTH:(g + 1) * GROUP_WIDTH]
        parts.append(v * lax.rsqrt(jnp.mean(v * v, axis=-1, keepdims=True) + RMS_EPS))
    return jnp.concatenate(parts, axis=1) * nw


def _ssd_kernel(xbc_ref, z_ref, dt_ref, cw_ref, cb_ref, alog_ref, dsk_ref, nw_ref, e_ref, tril_ref,
                y_ref, hout_ref, ext_ref, ht_ref):
    c = pl.program_id(1)
    L = SSM_CHUNK

    @pl.when(c == 0)
    def _():
        ext_ref[0:SUBLANES, :] = jnp.zeros((SUBLANES, CONV_DIM), F32)
        ht_ref[...] = jnp.zeros_like(ht_ref)

    ext_ref[SUBLANES:SUBLANES + L, :] = xbc_ref[...]
    conv = cb_ref[...] + cw_ref[SSM_CONV - 1:SSM_CONV, :] * xbc_ref[...]
    for k in range(1, SSM_CONV):
        conv = conv + cw_ref[SSM_CONV - 1 - k:SSM_CONV - k, :] * ext_ref[pl.ds(SUBLANES - k, L), :]
    ext_ref[0:SUBLANES, :] = ext_ref[L:L + SUBLANES, :]
    xc = _silu(conv)
    xs = xc[:, :SSM_WIDTH]
    bm = xc[:, SSM_WIDTH:SSM_WIDTH + SSM_GROUPS * SSM_STATE]
    cm = xc[:, SSM_WIDTH + SSM_GROUPS * SSM_STATE:]

    dt = dt_ref[...]
    da = dt * (-jnp.exp(alog_ref[...]))
    acs = _dot_exact_rhs(tril_ref[...], da)
    e = e_ref[...]
    acs_x = _dot_exact_lhs(acs, e)
    dt_x = _dot_exact_lhs(dt, e)
    last_x = acs_x[L - 1:L, :]
    xw = xs * dt_x * jnp.exp(last_x - acs_x)
    exp_acs = jnp.exp(acs_x)
    acs_t = acs.T
    dt_t = dt.T

    row = lax.broadcasted_iota(jnp.int32, (L, L), 0)
    col = lax.broadcasted_iota(jnp.int32, (L, L), 1)
    causal = row >= col
    lower_half = col < SSM_HEAD_DIM

    hprev = ht_ref[...].astype(BF16)
    y_parts = []
    heads_per_group = SSM_HEADS // SSM_GROUPS
    for g in range(SSM_GROUPS):
        gs = slice(g * GROUP_WIDTH, (g + 1) * GROUP_WIDTH)
        bg = bm[:, g * SSM_STATE:(g + 1) * SSM_STATE]
        cg = cm[:, g * SSM_STATE:(g + 1) * SSM_STATE].astype(BF16)
        cb = _dot_nt(cg, bg.astype(BF16))
        y_off = _dot(cg, hprev[:, gs]) * exp_acs[:, gs]
        for jj in range(heads_per_group // 2):
            j = g * (heads_per_group // 2) + jj
            xp = xs[:, j * LANES:(j + 1) * LANES]
            yp = None
            for h, xh in ((2 * j, jnp.where(lower_half, xp, 0.0)), (2 * j + 1, jnp.where(lower_half, 0.0, xp))):
                diff = acs[:, h:h + 1] - acs_t[h:h + 1, :]
                dec = jnp.exp(jnp.where(causal, diff, NEG_BIG))
                m = (cb * dec * dt_t[h:h + 1, :]).astype(BF16)
                t = _dot(m, xh.astype(BF16))
                yp = t if yp is None else yp + t
            y_parts.append(yp + y_off[:, jj * LANES:(jj + 1) * LANES])
        ht_ref[:, gs] = ht_ref[:, gs] * jnp.exp(last_x[:, gs]) + _dot(bg.T.astype(BF16), xw[:, gs].astype(BF16))

    y = jnp.concatenate(y_parts, axis=1) + dsk_ref[...] * xs
    y_ref[...] = _gate_and_norm(y, z_ref[...], nw_ref[...]).astype(BF16)

    @pl.when(c == pl.num_programs(1) - 1)
    def _():
        hout_ref[0] = ht_ref[...].T


def _ssd_prompt(xbc, z, dt, cw, cb, alog, dsk, nw, e, tril, b, t):
    nc = t // SSM_CHUNK
    L = SSM_CHUNK
    row = lambda w: pl.BlockSpec((L, w), lambda bi, c: (bi * nc + c, 0))
    return pl.pallas_call(
        _ssd_kernel,
        grid=(b, nc),
        in_specs=[row(CONV_DIM), row(SSM_WIDTH), row(LANES)] +
                 [_resident(a.shape) for a in (cw, cb, alog, dsk, nw, e, tril)],
        out_specs=[row(SSM_WIDTH), pl.BlockSpec((1, SSM_WIDTH, SSM_STATE), lambda bi, c: (bi, 0, 0))],
        out_shape=[jax.ShapeDtypeStruct((b * t, SSM_WIDTH), BF16),
                   jax.ShapeDtypeStruct((b, SSM_WIDTH, SSM_STATE), F32)],
        scratch_shapes=[pltpu.VMEM((L + SUBLANES, CONV_DIM), F32), pltpu.VMEM((SSM_STATE, SSM_WIDTH), F32)],
        compiler_params=_cparams("arbitrary", "arbitrary"),
        name="ssd_prompt",
    )(xbc, z, dt, cw, cb, alog, dsk, nw, e, tril)


def _ssd_step_kernel(xbc_ref, sc_ref, z_ref, dt_ref, cw_ref, cb_ref, alog_ref, dsk_ref, nw_ref, e_ref, st_ref,
                     y_ref, so_ref, xs_s, b_s, ct_s, xdt_s, da_s, yt_s):
    s = pl.program_id(0)
    n_s = pl.num_programs(0)

    @pl.when(s == 0)
    def _():
        conv = cb_ref[...] + cw_ref[SSM_CONV - 1:SSM_CONV, :] * xbc_ref[...]
        for k in range(SSM_CONV - 1):
            conv = conv + cw_ref[k:k + 1, :] * sc_ref[k]
        xc = _silu(conv)
        xs = xc[:, :SSM_WIDTH]
        xs_s[...] = xs
        b_s[...] = xc[:, SSM_WIDTH:SSM_WIDTH + SSM_GROUPS * SSM_STATE]
        ct_s[...] = xc[:, SSM_WIDTH + SSM_GROUPS * SSM_STATE:].T
        dt = dt_ref[...]
        e = e_ref[...]
        xdt_s[...] = (xs * _dot_exact_lhs(dt, e)).T
        da = jnp.exp(dt * (-jnp.exp(alog_ref[...])))
        da_s[...] = _dot_exact_lhs(da, e).T
        yt_s[...] = jnp.zeros_like(yt_s)

    n_samp = xs_s.shape[0]
    row = lax.broadcasted_iota(jnp.int32, (n_samp, SSM_STATE), 0)
    lane = lax.broadcasted_iota(jnp.int32, (SSM_STATE, n_samp), 1)
    pick_row = row == s
    decay = _dot_exact_lhs(da_s[...], pick_row.astype(BF16))
    b_row = b_s[pl.ds(s, 1), :]
    for g in range(SSM_GROUPS):
        gs = slice(g * GROUP_WIDTH, (g + 1) * GROUP_WIDTH)
        ns = slice(g * SSM_STATE, (g + 1) * SSM_STATE)
        eb = jnp.where(pick_row, b_row[:, ns], 0.0)
        eb_hi = eb.astype(BF16)
        eb_lo = (eb - eb_hi.astype(F32)).astype(BF16)
        upd = _dot_exact_lhs(xdt_s[gs, :], eb_hi) + _dot_exact_lhs(xdt_s[gs, :], eb_lo)
        h_new = st_ref[0, gs, :] * decay[gs, :] + upd
        so_ref[0, gs, :] = h_new
        ce = jnp.where(lane == s, ct_s[ns, :], 0.0).astype(BF16)
        yt_s[gs, :] += _dot(h_new.astype(BF16), ce)

    @pl.when(s == n_s - 1)
    def _():
        y = yt_s[...].T + dsk_ref[...] * xs_s[...]
        y_ref[...] = _gate_and_norm(y, z_ref[...], nw_ref[...]).astype(BF16)


def _ssd_step(xbc, sc, z, dt, cw, cb, alog, dsk, nw, e, state):
    n_s = xbc.shape[0]
    st_spec = pl.BlockSpec((1, SSM_WIDTH, SSM_STATE), lambda s: (s, 0, 0))
    return pl.pallas_call(
        _ssd_step_kernel,
        grid=(n_s,),
        in_specs=[_resident(a.shape) for a in (xbc, sc, z, dt, cw, cb, alog, dsk, nw, e)] + [st_spec],
        out_specs=[_resident((n_s, SSM_WIDTH)), st_spec],
        out_shape=[jax.ShapeDtypeStruct((n_s, SSM_WIDTH), BF16),
                   jax.ShapeDtypeStruct((n_s, SSM_WIDTH, SSM_STATE), F32)],
        scratch_shapes=[pltpu.VMEM((n_s, SSM_WIDTH), F32), pltpu.VMEM((n_s, SSM_GROUPS * SSM_STATE), F32),
                        pltpu.VMEM((SSM_GROUPS * SSM_STATE, n_s), F32), pltpu.VMEM((SSM_WIDTH, n_s), F32),
                        pltpu.VMEM((SSM_WIDTH, n_s), F32), pltpu.VMEM((SSM_WIDTH, n_s), F32)],
        compiler_params=_cparams("arbitrary"),
        name="ssd_step",
    )(xbc, sc, z, dt, cw, cb, alog, dsk, nw, e, state)


def _rope_q_head(qh, tab, lane):
    r = _rope_fold(qh[:, QK_NOPE:] * tab)
    return qh[:, :QK_NOPE], jnp.where(lane < QK_ROPE, r, 0.0)


def _qkv_kernel(cq_ref, ckv_ref, kpad_ref, tab_ref, wq_ref, wuk_ref, wuv_ref, q_ref, k_ref, v_ref):
    q = _dot(cq_ref[...], wq_ref[...])
    ckv = ckv_ref[...].astype(BF16)
    kn = _dot(ckv, wuk_ref[...])
    v = _dot(ckv, wuv_ref[...])
    tab = tab_ref[...]
    kpad = kpad_ref[...]
    lane = lax.broadcasted_iota(jnp.int32, tab.shape, 1)
    for h in range(MLA_HEADS):
        nope, pe = _rope_q_head(q[:, h * QK_PAD:(h + 1) * QK_PAD], tab, lane)
        q_ref[0, h] = (jnp.concatenate([nope, pe], axis=1) * EXP2_SCALE).astype(BF16)
        k_ref[0, h] = jnp.concatenate([kn[:, h * QK_NOPE:(h + 1) * QK_NOPE].astype(BF16), kpad], axis=1)
        vt = v[:, h * V_HEAD_DIM:(h + 1) * V_HEAD_DIM].T.astype(BF16)
        tkv = v_ref.shape[-1]
        for c in range(v_ref.shape[2]):
            v_ref[0, h, c] = vt[:, c * tkv:(c + 1) * tkv]


def _qkv_prompt(cq, ckv, kpad, tab, wq, wuk, wuv, b, t, tm, tkv):
    nt = t // tm
    nc = tm // tkv
    row = lambda w: pl.BlockSpec((tm, w), lambda bi, i: (bi * nt + i, 0))
    head = lambda w: pl.BlockSpec((1, MLA_HEADS, tm, w), lambda bi, i: (bi, 0, i, 0))
    return pl.pallas_call(
        _qkv_kernel,
        grid=(b, nt),
        in_specs=[row(Q_LORA), row(KV_LORA), row(LANES), pl.BlockSpec((tm, LANES), lambda bi, i: (i, 0)),
                  _resident(wq.shape), _resident(wuk.shape), _resident(wuv.shape)],
        out_specs=[head(QK_PAD), head(QK_PAD),
                   pl.BlockSpec((1, MLA_HEADS, nc, V_HEAD_DIM, tkv), lambda bi, i: (bi, 0, i, 0, 0))],
        out_shape=[jax.ShapeDtypeStruct((b, MLA_HEADS, t, QK_PAD), BF16),
                   jax.ShapeDtypeStruct((b, MLA_HEADS, t, QK_PAD), BF16),
                   jax.ShapeDtypeStruct((b, MLA_HEADS, t // tkv, V_HEAD_DIM, tkv), BF16)],
        compiler_params=_cparams("arbitrary", "arbitrary"),
        name="qkv_prompt",
    )(cq, ckv, kpad, tab, wq, wuk, wuv)


def _q_step_kernel(cq_ref, tab_ref, wq_ref, wukt_ref, ql_ref, qp_ref):
    q = _dot(cq_ref[...], wq_ref[...])
    tab = tab_ref[...]
    lane = lax.broadcasted_iota(jnp.int32, tab.shape, 1)
    for h in range(MLA_HEADS):
        nope, pe = _rope_q_head(q[:, h * QK_PAD:(h + 1) * QK_PAD], tab, lane)
        ql_ref[h] = _dot(nope.astype(BF16), wukt_ref[h]).astype(BF16)
        qp_ref[h] = pe.astype(BF16)


def _q_step(cq, tab, wq, wukt):
    n_s = cq.shape[0]
    return pl.pallas_call(
        _q_step_kernel,
        grid=(1,),
        in_specs=[_resident(a.shape) for a in (cq, tab, wq, wukt)],
        out_specs=[_resident((MLA_HEADS, n_s, KV_LORA)), _resident((MLA_HEADS, n_s, LANES))],
        out_shape=[jax.ShapeDtypeStruct((MLA_HEADS, n_s, KV_LORA), BF16),
                   jax.ShapeDtypeStruct((MLA_HEADS, n_s, LANES), BF16)],
        compiler_params=_cparams("arbitrary"),
        name="q_step",
    )(cq, tab, wq, wukt)


def _softmax_update(s, m_ref, l_ref):
    m_prev = m_ref[...]
    m_new = jnp.maximum(m_prev, jnp.max(s, axis=-1, keepdims=True))
    corr = jnp.exp2((m_prev - m_new) * EXP2_SCALE)
    p = jnp.exp2((s - m_new) * EXP2_SCALE)
    l_ref[...] = l_ref[...] * corr + jnp.sum(p, axis=-1, keepdims=True)
    m_ref[...] = m_new
    return p, corr


def _flash_kernel(q_ref, k_ref, vt_ref, o_ref, *scratch, tq, nh):
    m_s, l_s, acc_s = scratch[:nh], scratch[nh:2 * nh], scratch[2 * nh:]
    qi = pl.program_id(2)
    for h in range(nh):
        m_s[h][...] = jnp.full_like(m_s[h], NEG_BIG)
        l_s[h][...] = jnp.zeros_like(l_s[h])
        acc_s[h][...] = jnp.zeros_like(acc_s[h])

    def step(j, masked):
        start = pl.multiple_of(j * tq, tq)
        sts = [_dot_nt(k_ref[0, h, pl.ds(start, tq), :], q_ref[0, h]) for h in range(nh)]
        ps, corrs = [], []
        for h in range(nh):
            st = sts[h]
            if masked:
                key = lax.broadcasted_iota(jnp.int32, st.shape, 0)
                qry = lax.broadcasted_iota(jnp.int32, st.shape, 1)
                st = jnp.where(key <= qry, st, NEG_BIG)
            m_prev = m_s[h][...]
            m_new = jnp.maximum(m_prev, jnp.max(st, axis=0, keepdims=True))
            corr = jnp.exp2(m_prev - m_new)
            p = jnp.exp2(st - m_new)
            l_s[h][...] = l_s[h][...] * corr + jnp.sum(p, axis=0, keepdims=True)
            m_s[h][...] = m_new
            ps.append(p.astype(BF16))
            corrs.append(corr)
        for h in range(nh):
            acc_s[h][...] = acc_s[h][...] * corrs[h] + _dot(vt_ref[0, h, j], ps[h])

    def body(j, carry):
        step(j, False)
        return carry

    lax.fori_loop(0, qi, body, 0)
    step(qi, True)
    for h in range(nh):
        o_ref[0, :, h * V_HEAD_DIM:(h + 1) * V_HEAD_DIM] = (acc_s[h][...] / l_s[h][...]).T.astype(BF16)


def _flash_prompt(q, k, vt, tq):
    b, h, t, _ = q.shape
    nh = FLASH_HEADS
    return pl.pallas_call(
        functools.partial(_flash_kernel, tq=tq, nh=nh),
        grid=(b, h // nh, t // tq),
        in_specs=[pl.BlockSpec((1, nh, tq, QK_PAD), lambda bi, hi, qi: (bi, hi, qi, 0)),
                  pl.BlockSpec((1, nh, t, QK_PAD), lambda bi, hi, qi: (bi, hi, 0, 0)),
                  pl.BlockSpec((1, nh, t // tq, V_HEAD_DIM, tq), lambda bi, hi, qi: (bi, hi, 0, 0, 0))],
        out_specs=pl.BlockSpec((1, tq, nh * V_HEAD_DIM), lambda bi, hi, qi: (bi, qi, hi)),
        out_shape=jax.ShapeDtypeStruct((b, t, h * V_HEAD_DIM), BF16),
        scratch_shapes=([pltpu.VMEM((1, tq), F32)] * (2 * nh) + [pltpu.VMEM((V_HEAD_DIM, tq), F32)] * nh),
        compiler_params=_cparams("arbitrary", "arbitrary", "arbitrary"),
        name="flash_prompt",
    )(q, k, vt)


def _decode_kernel(pt_ref, ql_ref, qp_ref, cn_ref, kn_ref, ckv_hbm, kpe_hbm, o_ref,
                   kc_buf, kp_buf, sem, m_s, l_s, acc_s, *, n_pg, n_groups, page):
    s = pl.program_id(0)
    n_s = pl.num_programs(0)

    def copies(samp, grp, slot):
        out = []
        for i in range(n_pg):
            pg = pt_ref[samp, grp * n_pg + i]
            out.append(pltpu.make_async_copy(ckv_hbm.at[pg], kc_buf.at[slot, pl.ds(i * page, page), :],
                                             sem.at[0, slot]))
            out.append(pltpu.make_async_copy(kpe_hbm.at[pg], kp_buf.at[slot, :, pl.ds(i * page, page)],
                                             sem.at[1, slot]))
        return out

    def start(samp, grp, slot):
        for c in copies(samp, grp, slot):
            c.start()

    @pl.when(s == 0)
    def _():
        start(0, 0, 0)

    ql = ql_ref[0]
    qp = qp_ref[0][:, :QK_ROPE]
    cn = cn_ref[...].astype(BF16).astype(F32)
    kn = kn_ref[...].astype(BF16).astype(F32)
    m_s[...] = (jnp.sum(ql.astype(F32) * cn, axis=-1, keepdims=True)
                + jnp.sum(qp_ref[0].astype(F32) * kn, axis=-1, keepdims=True))
    l_s[...] = jnp.ones_like(l_s)
    acc_s[...] = jnp.broadcast_to(cn, acc_s.shape)

    def attend(slot):
        kc = kc_buf[slot].astype(BF16)
        kpt = kp_buf[slot].astype(BF16)
        sc = _dot_nt(ql, kc) + _dot(qp, kpt)
        p, corr = _softmax_update(sc, m_s, l_s)
        acc_s[...] = acc_s[...] * corr + _dot(p.astype(BF16), kc)

    def pair(gp, carry):
        grp = 2 * gp
        start(s, grp + 1, 1)
        for c in copies(s, grp, 0):
            c.wait()
        attend(0)

        @pl.when(grp + 2 < n_groups)
        def _():
            start(s, grp + 2, 0)

        @pl.when(jnp.logical_and(grp + 2 == n_groups, s + 1 < n_s))
        def _():
            start(s + 1, 0, 0)

        for c in copies(s, grp + 1, 1):
            c.wait()
        attend(1)
        return carry

    lax.fori_loop(0, n_groups // 2, pair, 0)
    o_ref[0] = acc_s[...] / l_s[...]


def _decode_attention(page_table, ql, qp, ckv_new, kpe_new, ckv_pool, kpe_pool_t):
    n_s, n_pages = page_table.shape
    page = ckv_pool.shape[1]
    n_pg = min(DECODE_PAGES, n_pages // 2)
    assert n_pages % (2 * n_pg) == 0
    grid_spec = pltpu.PrefetchScalarGridSpec(
        num_scalar_prefetch=1,
        grid=(n_s,),
        in_specs=[pl.BlockSpec((1, HEAD_PAD, KV_LORA), lambda s, pt: (s, 0, 0)),
                  pl.BlockSpec((1, HEAD_PAD, LANES), lambda s, pt: (s, 0, 0)),
                  pl.BlockSpec((None, 1, KV_LORA), lambda s, pt: (s, 0, 0)),
                  pl.BlockSpec((None, 1, LANES), lambda s, pt: (s, 0, 0)),
                  pl.BlockSpec(memory_space=pl.ANY), pl.BlockSpec(memory_space=pl.ANY)],
        out_specs=pl.BlockSpec((1, HEAD_PAD, KV_LORA), lambda s, pt: (s, 0, 0)),
        scratch_shapes=[pltpu.VMEM((2, n_pg * page, KV_LORA), F32), pltpu.VMEM((2, QK_ROPE, n_pg * page), F32),
                        pltpu.SemaphoreType.DMA((2, 2)),
                        pltpu.VMEM((HEAD_PAD, 1), F32), pltpu.VMEM((HEAD_PAD, 1), F32),
                        pltpu.VMEM((HEAD_PAD, KV_LORA), F32)],
    )
    return pl.pallas_call(
        functools.partial(_decode_kernel, n_pg=n_pg, n_groups=n_pages // n_pg, page=page),
        grid_spec=grid_spec,
        out_shape=jax.ShapeDtypeStruct((n_s, HEAD_PAD, KV_LORA), F32),
        compiler_params=_cparams("arbitrary"),
        name="decode_attention",
    )(page_table, ql, qp, ckv_new, kpe_new, ckv_pool, kpe_pool_t)


def _uv_kernel(o_ref, wuv_ref, y_ref):
    for h in range(MLA_HEADS):
        y_ref[:, h * V_HEAD_DIM:(h + 1) * V_HEAD_DIM] = _dot(o_ref[h], wuv_ref[h]).astype(BF16)


def _uv_step(o, wuv):
    n_s = o.shape[1]
    return pl.pallas_call(
        _uv_kernel,
        grid=(1,),
        in_specs=[_resident(o.shape), _resident(wuv.shape)],
        out_specs=_resident((n_s, MLA_HEADS * V_HEAD_DIM)),
        out_shape=jax.ShapeDtypeStruct((n_s, MLA_HEADS * V_HEAD_DIM), BF16),
        compiler_params=_cparams("arbitrary"),
        name="uv_step",
    )(o, wuv)


def _wo_ln_kernel(ys_ref, ym_ref, x_ref, wo_ref, g_ref, b_ref, h_ref):
    mix = _dot(ys_ref[...], wo_ref[0:SSM_WIDTH, :]) + _dot(ym_ref[...], wo_ref[SSM_WIDTH:, :])
    h_ref[...] = _layer_norm(ALPHA * x_ref[...] + mix, g_ref[...], b_ref[...])


def _wo_ln(ys, ym, x, wo, g, b, tm):
    n = x.shape[0]
    row = lambda w: pl.BlockSpec((tm, w), lambda i: (i, 0))
    return pl.pallas_call(
        _wo_ln_kernel,
        grid=(n // tm,),
        in_specs=[row(SSM_WIDTH), row(SSM_WIDTH), row(D_MODEL), _resident(wo.shape), _resident(g.shape),
                  _resident(b.shape)],
        out_specs=row(D_MODEL),
        out_shape=jax.ShapeDtypeStruct((n, D_MODEL), F32),
        compiler_params=_cparams("arbitrary"),
        name="wo_ln",
    )(ys, ym, x, wo, g, b)


def _ffn_tail(j, h_ref, act, wd_ref, g2_ref, b2_ref, y_ref, acc_s):
    acc_s[...] += _dot(act.astype(BF16), wd_ref[...])

    @pl.when(j == pl.num_programs(1) - 1)
    def _():
        y_ref[...] = _layer_norm(ALPHA * h_ref[...] + acc_s[...], g2_ref[...], b2_ref[...])


def _ffn_head(j, h_ref, wg_ref, wu_ref, hb_s, acc_s):
    @pl.when(j == 0)
    def _():
        hb_s[...] = h_ref[...].astype(BF16)
        acc_s[...] = jnp.zeros_like(acc_s)

    hb = hb_s[...]
    return _dot(hb, wg_ref[...]), _dot(hb, wu_ref[...])


def _ffn_prompt_kernel(h_ref, wg_ref, wu_ref, wd_ref, cw_ref, cb_ref, g2_ref, b2_ref, y_ref, gl_ref,
                       hb_s, acc_s, ext_s, carry_s, *, blocks_per_seq):
    i = pl.program_id(0)
    j = pl.program_id(1)
    tm = h_ref.shape[0]
    g, u = _ffn_head(j, h_ref, wg_ref, wu_ref, hb_s, acc_s)

    @pl.when(i % blocks_per_seq == 0)
    def _():
        carry_s[j] = jnp.zeros(carry_s.shape[1:], F32)

    ext_s[0:SUBLANES, :] = carry_s[j]
    ext_s[SUBLANES:, :] = g
    gc = cb_ref[...] + cw_ref[FFN_CONV - 1:FFN_CONV, :] * g
    for k in range(1, FFN_CONV):
        gc = gc + cw_ref[FFN_CONV - 1 - k:FFN_CONV - k, :] * ext_s[pl.ds(SUBLANES - k, tm), :]
    tail = ext_s[tm:tm + SUBLANES, :]
    carry_s[j] = tail
    gl_ref[0] = tail
    _ffn_tail(j, h_ref, _silu(gc) * u, wd_ref, g2_ref, b2_ref, y_ref, acc_s)


def _ffn_step_kernel(h_ref, wg_ref, wu_ref, wd_ref, cw_ref, cb_ref, p2_ref, p1_ref, g2_ref, b2_ref,
                     y_ref, gout_ref, hb_s, acc_s):
    j = pl.program_id(1)
    g, u = _ffn_head(j, h_ref, wg_ref, wu_ref, hb_s, acc_s)
    gout_ref[...] = g
    gc = cb_ref[...] + cw_ref[0:1, :] * p2_ref[...] + cw_ref[1:2, :] * p1_ref[...] + cw_ref[2:3, :] * g
    _ffn_tail(j, h_ref, _silu(gc) * u, wd_ref, g2_ref, b2_ref, y_ref, acc_s)


def _ffn_specs(tm, tf):
    nj = D_FF // tf
    return [pl.BlockSpec((tm, D_MODEL), lambda i, j: (i, 0)),
            pl.BlockSpec((D_MODEL, tf), lambda i, j: (0, j)),
            pl.BlockSpec((D_MODEL, tf), lambda i, j: (0, j + nj)),
            pl.BlockSpec((tf, D_MODEL), lambda i, j: (j, 0)),
            pl.BlockSpec((FFN_CONV, tf), lambda i, j: (0, j)),
            pl.BlockSpec((1, tf), lambda i, j: (0, j))]


def _ffn_prompt(h, w_in, w_down, cw, cb, g2, b2, b, t, tm, tf):
    n = h.shape[0]
    nj = D_FF // tf
    bps = t // tm
    vec = pl.BlockSpec((1, D_MODEL), lambda i, j: (0, 0))
    return pl.pallas_call(
        functools.partial(_ffn_prompt_kernel, blocks_per_seq=bps),
        grid=(n // tm, nj),
        in_specs=_ffn_specs(tm, tf) + [vec, vec],
        out_specs=[pl.BlockSpec((tm, D_MODEL), lambda i, j: (i, 0)),
                   pl.BlockSpec((1, SUBLANES, tf), lambda i, j: (i, 0, j))],
        out_shape=[jax.ShapeDtypeStruct((n, D_MODEL), F32), jax.ShapeDtypeStruct((n // tm, SUBLANES, D_FF), F32)],
        scratch_shapes=[pltpu.VMEM((tm, D_MODEL), BF16), pltpu.VMEM((tm, D_MODEL), F32),
                        pltpu.VMEM((tm + SUBLANES, tf), F32), pltpu.VMEM((nj, SUBLANES, tf), F32)],
        compiler_params=_cparams("arbitrary", "arbitrary"),
        name="ffn_prompt",
    )(h, w_in, w_in, w_down, cw, cb, g2, b2)


def _ffn_step(h, w_in, w_down, cw, cb, p2, p1, g2, b2, tf):
    n = h.shape[0]
    nj = D_FF // tf
    vec = pl.BlockSpec((1, D_MODEL), lambda i, j: (0, 0))
    col = pl.BlockSpec((n, tf), lambda i, j: (0, j))
    return pl.pallas_call(
        _ffn_step_kernel,
        grid=(1, nj),
        in_specs=_ffn_specs(n, tf) + [col, col, vec, vec],
        out_specs=[pl.BlockSpec((n, D_MODEL), lambda i, j: (0, 0)), col],
        out_shape=[jax.ShapeDtypeStruct((n, D_MODEL), F32), jax.ShapeDtypeStruct((n, D_FF), F32)],
        scratch_shapes=[pltpu.VMEM((n, D_MODEL), BF16), pltpu.VMEM((n, D_MODEL), F32)],
        compiler_params=_cparams("arbitrary", "arbitrary"),
        name="ffn_step",
    )(h, w_in, w_in, w_down, cw, cb, p2, p1, g2, b2)


def _rope_table(pos):
    inv_freq = ROPE_THETA ** (-jnp.arange(0, QK_ROPE, 2, dtype=F32) / QK_ROPE)
    ang = pos.astype(F32)[:, None] * inv_freq[None, :]
    c, s = jnp.cos(ang), jnp.sin(ang)
    return jnp.concatenate([c, c, -s, s], axis=1)


def _swap_halves(w):
    half = w.shape[-1] // 2
    return jnp.concatenate([w[..., half:], w[..., :half]], axis=-1)


def _pad_lanes(v):
    return jnp.pad(v.reshape(1, -1).astype(F32), ((0, 0), (0, LANES - v.shape[-1])))


def _tile(n, cap):
    t = min(n, cap)
    assert n % t == 0
    return t


def kernel(x_prompt, x_sample, cache_ckv, cache_kpe, page_table, state_ssm, state_conv, state_ffn_conv,
           w_in, conv_w, conv_b, dt_bias, a_log, d_skip, ssm_norm_w, q_norm_w, kv_norm_w,
           w_uq, w_uk, w_uv, w_o, ln1_g, ln1_b, w_ffn_in, ffn_conv_w, ffn_conv_b, w_ffn_down, ln2_g, ln2_b):
    assert w_in.shape[0] == DEPTH == 1 and x_sample.shape[1] == 1
    b, t, _ = x_prompt.shape
    n_s = x_sample.shape[0]
    n_pages = page_table.shape[1]
    past_len = n_pages * cache_ckv.shape[2]
    assert t % SSM_CHUNK == 0

    wi = w_in[0]
    w_kpe = wi[:, OFF_CKV:]
    w1 = jnp.concatenate([wi[:, :OFF_XBC], wi[:, OFF_DT:OFF_CKV], w_kpe, _swap_halves(w_kpe),
                          wi[:, OFF_XBC:OFF_DT], jnp.zeros((D_MODEL, LANES - SSM_HEADS), F32)], axis=1).astype(BF16)
    uq = w_uq[0]
    uq_pe = uq[:, :, QK_NOPE:]
    wq = jnp.concatenate([uq[:, :, :QK_NOPE], uq_pe, _swap_halves(uq_pe)], axis=-1)
    wq = wq.reshape(Q_LORA, MLA_HEADS * QK_PAD).astype(BF16)
    wuk = w_uk[0].reshape(KV_LORA, MLA_HEADS * QK_NOPE).astype(BF16)
    wuv = w_uv[0].reshape(KV_LORA, MLA_HEADS * V_HEAD_DIM).astype(BF16)
    wukt = jnp.transpose(w_uk[0], (1, 2, 0)).astype(BF16)
    wuv_h = jnp.transpose(w_uv[0], (1, 0, 2)).astype(BF16)
    wo = w_o[0].astype(BF16)
    wf_in = w_ffn_in[0].astype(BF16)
    wf_down = w_ffn_down[0].astype(BF16)
    row = lambda v: v.reshape(1, -1).astype(F32)
    dtb, alog = _pad_lanes(dt_bias[0]), _pad_lanes(a_log[0])
    dsk = row(jnp.repeat(d_skip[0], SSM_HEAD_DIM))
    nw, qg, kg = row(ssm_norm_w[0]), row(q_norm_w[0]), row(kv_norm_w[0])
    cw, cb = conv_w[0], row(conv_b[0])
    fcw, fcb = ffn_conv_w[0], row(ffn_conv_b[0])
    g1, b1, g2, b2 = row(ln1_g[0]), row(ln1_b[0]), row(ln2_g[0]), row(ln2_b[0])
    expand = (jnp.arange(SSM_WIDTH)[None, :] // SSM_HEAD_DIM == jnp.arange(LANES)[:, None]).astype(BF16)
    tril = (jnp.arange(SSM_CHUNK)[:, None] >= jnp.arange(SSM_CHUNK)[None, :]).astype(BF16)
    tab_p = _rope_table(jnp.arange(t))
    tab_s = _rope_table(jnp.full((n_s,), past_len))

    tm = _tile(t, ROW_TILE)
    xp = x_prompt.reshape(b * t, D_MODEL)
    z, xbc, cq, ckv, kpe, kpad, dt = _in_proj(xp, w1, tab_p, dtb, qg, kg, tm)
    y_ssd, h_fin = _ssd_prompt(xbc, z, dt, cw, cb, alog, dsk, nw, expand, tril, b, t)
    tq = _tile(tm, FLASH_TILE)
    q, k, vt = _qkv_prompt(cq, ckv, kpad, tab_p, wq, wuk, wuv, b, t, tm, tq)
    y_mla = _flash_prompt(q, k, vt, tq).reshape(b * t, MLA_HEADS * V_HEAD_DIM)
    h1 = _wo_ln(y_ssd, y_mla, xp, wo, g1, b1, tm)
    tf = FF_TILE
    y_p, g_last = _ffn_prompt(h1, wf_in, wf_down, fcw, fcb, g2, b2, b, t, tm, tf)

    xs_ = x_sample.reshape(n_s, D_MODEL)
    z_s, xbc_s, cq_s, ckv_s, kpe_s, kpad_s, dt_s = _in_proj(xs_, w1, tab_s, dtb, qg, kg, n_s)
    sc = jnp.transpose(state_conv[0], (1, 0, 2))
    y_ssd_s, st_new = _ssd_step(xbc_s, sc, z_s, dt_s, cw, cb, alog, dsk, nw, expand,
                                state_ssm[0].reshape(n_s, SSM_WIDTH, SSM_STATE))
    ql, qp = _q_step(cq_s, tab_s, wq, wukt)
    pad_heads = lambda a: jnp.pad(jnp.transpose(a, (1, 0, 2)), ((0, 0), (0, HEAD_PAD - MLA_HEADS), (0, 0)))
    o_lat = _decode_attention(page_table, pad_heads(ql), pad_heads(qp), ckv_s.reshape(n_s, 1, KV_LORA),
                              kpad_s.astype(F32).reshape(n_s, 1, LANES), cache_ckv[0],
                              jnp.swapaxes(cache_kpe[0], 1, 2))
    y_mla_s = _uv_step(jnp.transpose(o_lat[:, :MLA_HEADS], (1, 0, 2)).astype(BF16), wuv_h)
    h1_s = _wo_ln(y_ssd_s, y_mla_s, xs_, wo, g1, b1, n_s)
    fbuf = state_ffn_conv[0]
    y_s, g_s = _ffn_step(h1_s, wf_in, wf_down, fcw, fcb, fbuf[:, 0], fbuf[:, 1], g2, b2, tf)

    lead = lambda a: a[None]
    return (y_p.reshape(b, t, D_MODEL),
            y_s.reshape(n_s, 1, D_MODEL),
            lead(ckv.reshape(b, t, KV_LORA)),
            lead(kpe.reshape(b, t, QK_ROPE)),
            lead(h_fin.reshape(b, SSM_HEADS, SSM_HEAD_DIM, SSM_STATE)),
            lead(xbc.reshape(b, t, CONV_DIM)[:, t - (SSM_CONV - 1):]),
            lead(g_last.reshape(b, t // tm, SUBLANES, D_FF)[:, -1, SUBLANES - (FFN_CONV - 1):]),
            lead(ckv_s.reshape(n_s, 1, KV_LORA)),
            lead(kpe_s.reshape(n_s, 1, QK_ROPE)),
            lead(st_new.reshape(n_s, SSM_HEADS, SSM_HEAD_DIM, SSM_STATE)),
            lead(jnp.concatenate([state_conv[0][:, 1:], xbc_s[:, None]], axis=1)),
            lead(jnp.concatenate([fbuf[:, 1:], g_s[:, None]], axis=1)))
```

```python
import functools
import math

import jax
import jax.numpy as jnp
from jax import lax
from jax.experimental import pallas as pl
from jax.experimental.pallas import tpu as pltpu

F32 = jnp.float32
BF16 = jnp.bfloat16

D_MODEL = 2048
SSM_WIDTH = 1024
SSM_HEAD_DIM = 64
SSM_HEADS = 16
SSM_GROUPS = 2
SSM_STATE = 128
SSM_CONV = 4
SSM_CHUNK = 128
CONV_DIM = SSM_WIDTH + 2 * SSM_GROUPS * SSM_STATE
GROUP_WIDTH = SSM_WIDTH // SSM_GROUPS
MLA_HEADS = 8
V_HEAD_DIM = 128
QK_NOPE = 128
QK_ROPE = 64
Q_LORA = 512
KV_LORA = 512
ROPE_THETA = 10000.0
ATTN_SCALE = (QK_NOPE + QK_ROPE) ** -0.5
D_FF = 5632
FFN_CONV = 3
LN_EPS = 1e-5
RMS_EPS = 1e-6
DEPTH = 1
ALPHA = (2.0 * DEPTH) ** 0.25
OFF_Z = SSM_WIDTH
OFF_XBC = OFF_Z + CONV_DIM
OFF_DT = OFF_XBC + SSM_HEADS
OFF_CQ = OFF_DT + Q_LORA
OFF_CKV = OFF_CQ + KV_LORA

LANES = 128
SUBLANES = 8
QK_PAD = 2 * LANES
HEAD_PAD = 16
NEG_BIG = -1e30
VMEM_LIMIT = 56 * 1024 * 1024
ROW_TILE = 512
FF_TILE = 512
FFN_PARTS = 2
FLASH_HEADS = 4
FLASH_TILE = 512
DECODE_PAGES = 16
DECODE_SLOTS = 3
EXP2_SCALE = ATTN_SCALE * math.log2(math.e)


def _cparams(*sem):
    return pltpu.CompilerParams(dimension_semantics=sem, vmem_limit_bytes=VMEM_LIMIT)


def _dot(a, b):
    return jnp.dot(a, b, preferred_element_type=F32)


def _dot_nt(a, b):
    return lax.dot_general(a, b, (((1,), (1,)), ((), ())), preferred_element_type=F32)


def _split3(x):
    hi = x.astype(BF16)
    r = x - hi.astype(F32)
    mid = r.astype(BF16)
    lo = (r - mid.astype(F32)).astype(BF16)
    return hi, mid, lo


def _dot_exact_lhs(x, sel):
    hi, mid, lo = _split3(x)
    return _dot(hi, sel) + _dot(mid, sel) + _dot(lo, sel)


def _dot_exact_rhs(sel, x):
    hi, mid, lo = _split3(x)
    return _dot(sel, hi) + _dot(sel, mid) + _dot(sel, lo)


def _silu(x):
    hx = 0.5 * x
    return hx + hx * jnp.tanh(hx)


def _softplus(x):
    return jnp.maximum(x, 0.0) + jnp.log1p(jnp.exp(-jnp.abs(x)))


def _rms(x, g):
    r = lax.rsqrt(jnp.mean(x * x, axis=-1, keepdims=True) + RMS_EPS)
    return x * r * g


def _layer_norm(v, g, b):
    mu = jnp.mean(v, axis=-1, keepdims=True)
    d = v - mu
    var = jnp.mean(d * d, axis=-1, keepdims=True)
    return d * lax.rsqrt(var + LN_EPS) * g + b


def _rope_fold(t):
    return t + pltpu.roll(t, QK_ROPE, 1)


def _resident(shape):
    nd = len(shape)
    return pl.BlockSpec(shape, lambda *_: (0,) * nd)


C_Z = 0
C_XBC = C_Z + SSM_WIDTH
C_CQ = C_XBC + CONV_DIM
C_CKV = C_CQ + Q_LORA
C_KPE = C_CKV + KV_LORA
C_DT = C_KPE + LANES
C_END = C_DT + LANES


def _in_proj_kernel(x_ref, w_ref, tab_ref, dtb_ref, qg_ref, kg_ref,
                    z_ref, xbc_ref, cq_ref, ckv_ref, kpe_ref, kpad_ref, dt_ref):
    xb = x_ref[...].astype(BF16)
    z_ref[...] = _dot(xb, w_ref[:, C_Z:C_XBC])
    xbc_ref[...] = _dot(xb, w_ref[:, C_XBC:C_CQ])
    cq_ref[...] = _rms(_dot(xb, w_ref[:, C_CQ:C_CKV]), qg_ref[...]).astype(BF16)
    ckv_ref[...] = _rms(_dot(xb, w_ref[:, C_CKV:C_KPE]), kg_ref[...])
    kr = _rope_fold(_dot(xb, w_ref[:, C_KPE:C_DT]) * tab_ref[...])
    kpe_ref[...] = kr[:, :QK_ROPE]
    lane = lax.broadcasted_iota(jnp.int32, kr.shape, 1)
    kpad_ref[...] = jnp.where(lane < QK_ROPE, kr, 0.0).astype(BF16)
    dt_ref[...] = _softplus(_dot(xb, w_ref[:, C_DT:C_END]) + dtb_ref[...])


def _in_proj(x, w1, tab, dtb, qg, kg, tm):
    n = x.shape[0]
    n_tab = tab.shape[0] // tm
    row = lambda w: pl.BlockSpec((tm, w), lambda i: (i, 0))
    return pl.pallas_call(
        _in_proj_kernel,
        grid=(n // tm,),
        in_specs=[row(D_MODEL), _resident(w1.shape),
                  pl.BlockSpec((tm, LANES), lambda i: (i % n_tab, 0)),
                  _resident(dtb.shape), _resident(qg.shape), _resident(kg.shape)],
        out_specs=[row(SSM_WIDTH), row(CONV_DIM), row(Q_LORA), row(KV_LORA), row(QK_ROPE), row(LANES),
                   row(LANES)],
        out_shape=[jax.ShapeDtypeStruct((n, SSM_WIDTH), F32), jax.ShapeDtypeStruct((n, CONV_DIM), F32),
                   jax.ShapeDtypeStruct((n, Q_LORA), BF16), jax.ShapeDtypeStruct((n, KV_LORA), F32),
                   jax.ShapeDtypeStruct((n, QK_ROPE), F32), jax.ShapeDtypeStruct((n, LANES), BF16),
                   jax.ShapeDtypeStruct((n, LANES), F32)],
        compiler_params=_cparams("arbitrary"),
        name="in_proj",
    )(x, w1, tab, dtb, qg, kg)


def _gate_and_norm(y, z, nw):
    yg = y * _silu(z)
    parts = []
    for g in range(SSM_GROUPS):
        v = yg[:, g * GROUP_WIDTH:(g + 1) * GROUP_WIDTH]
        parts.append(v * lax.rsqrt(jnp.mean(v * v, axis=-1, keepdims=True) + RMS_EPS))
    return jnp.concatenate(parts, axis=1) * nw


def _ssd_kernel(xbc_ref, z_ref, dt_ref, cw_ref, cb_ref, alog_ref, dsk_ref, nw_ref, e_ref, tril_ref,
                y_ref, hout_ref, ext_ref, ht_ref):
    c = pl.program_id(1)
    L = SSM_CHUNK

    @pl.when(c == 0)
    def _():
        ext_ref[0:SUBLANES, :] = jnp.zeros((SUBLANES, CONV_DIM), F32)
        ht_ref[...] = jnp.zeros_like(ht_ref)

    ext_ref[SUBLANES:SUBLANES + L, :] = xbc_ref[...]
    conv = cb_ref[...] + cw_ref[SSM_CONV - 1:SSM_CONV, :] * xbc_ref[...]
    for k in range(1, SSM_CONV):
        conv = conv + cw_ref[SSM_CONV - 1 - k:SSM_CONV - k, :] * ext_ref[pl.ds(SUBLANES - k, L), :]
    ext_ref[0:SUBLANES, :] = ext_ref[L:L + SUBLANES, :]
    xc = _silu(conv)
    xs = xc[:, :SSM_WIDTH]
    bm = xc[:, SSM_WIDTH:SSM_WIDTH + SSM_GROUPS * SSM_STATE]
    cm = xc[:, SSM_WIDTH + SSM_GROUPS * SSM_STATE:]

    dt = dt_ref[...]
    da = dt * (-jnp.exp(alog_ref[...]))
    acs = _dot_exact_rhs(tril_ref[...], da)
    e = e_ref[...]
    acs_x = _dot_exact_lhs(acs, e)
    dt_x = _dot_exact_lhs(dt, e)
    last_x = acs_x[L - 1:L, :]
    xw = xs * dt_x * jnp.exp(last_x - acs_x)
    exp_acs = jnp.exp(acs_x)
    acs_t = acs.T
    dt_t = dt.T

    row = lax.broadcasted_iota(jnp.int32, (L, L), 0)
    col = lax.broadcasted_iota(jnp.int32, (L, L), 1)
    causal = row >= col
    lower_half = col < SSM_HEAD_DIM

    hprev = ht_ref[...].astype(BF16)
    y_parts = []
    heads_per_group = SSM_HEADS // SSM_GROUPS
    for g in range(SSM_GROUPS):
        gs = slice(g * GROUP_WIDTH, (g + 1) * GROUP_WIDTH)
        bg = bm[:, g * SSM_STATE:(g + 1) * SSM_STATE]
        cg = cm[:, g * SSM_STATE:(g + 1) * SSM_STATE].astype(BF16)
        cb = _dot_nt(cg, bg.astype(BF16))
        y_off = _dot(cg, hprev[:, gs]) * exp_acs[:, gs]
        for jj in range(heads_per_group // 2):
            j = g * (heads_per_group // 2) + jj
            xp = xs[:, j * LANES:(j + 1) * LANES]
            yp = None
            for h, xh in ((2 * j, jnp.where(lower_half, xp, 0.0)), (2 * j + 1, jnp.where(lower_half, 0.0, xp))):
                diff = acs[:, h:h + 1] - acs_t[h:h + 1, :]
                dec = jnp.exp(jnp.where(causal, diff, NEG_BIG))
                m = (cb * dec * dt_t[h:h + 1, :]).astype(BF16)
                t = _dot(m, xh.astype(BF16))
                yp = t if yp is None else yp + t
            y_parts.append(yp + y_off[:, jj * LANES:(jj + 1) * LANES])
        ht_ref[:, gs] = ht_ref[:, gs] * jnp.exp(last_x[:, gs]) + _dot(bg.T.astype(BF16), xw[:, gs].astype(BF16))

    y = jnp.concatenate(y_parts, axis=1) + dsk_ref[...] * xs
    y_ref[...] = _gate_and_norm(y, z_ref[...], nw_ref[...]).astype(BF16)

    @pl.when(c == pl.num_programs(1) - 1)
    def _():
        hout_ref[0] = ht_ref[...].T


def _ssd_prompt(xbc, z, dt, cw, cb, alog, dsk, nw, e, tril, b, t):
    nc = t // SSM_CHUNK
    L = SSM_CHUNK
    row = lambda w: pl.BlockSpec((L, w), lambda bi, c: (bi * nc + c, 0))
    return pl.pallas_call(
        _ssd_kernel,
        grid=(b, nc),
        in_specs=[row(CONV_DIM), row(SSM_WIDTH), row(LANES)] +
                 [_resident(a.shape) for a in (cw, cb, alog, dsk, nw, e, tril)],
        out_specs=[row(SSM_WIDTH), pl.BlockSpec((1, SSM_WIDTH, SSM_STATE), lambda bi, c: (bi, 0, 0))],
        out_shape=[jax.ShapeDtypeStruct((b * t, SSM_WIDTH), BF16),
                   jax.ShapeDtypeStruct((b, SSM_WIDTH, SSM_STATE), F32)],
        scratch_shapes=[pltpu.VMEM((L + SUBLANES, CONV_DIM), F32), pltpu.VMEM((SSM_STATE, SSM_WIDTH), F32)],
        compiler_params=_cparams("arbitrary", "arbitrary"),
        name="ssd_prompt",
    )(xbc, z, dt, cw, cb, alog, dsk, nw, e, tril)


def _ssd_step_kernel(xbc_ref, sc_ref, z_ref, dt_ref, cw_ref, cb_ref, alog_ref, dsk_ref, nw_ref, e_ref, st_ref,
                     y_ref, so_ref, xs_s, b_s, ct_s, xdt_s, da_s, yt_s):
    s = pl.program_id(0)
    n_s = pl.num_programs(0)

    @pl.when(s == 0)
    def _():
        conv = cb_ref[...] + cw_ref[SSM_CONV - 1:SSM_CONV, :] * xbc_ref[...]
        for k in range(SSM_CONV - 1):
            conv = conv + cw_ref[k:k + 1, :] * sc_ref[k]
        xc = _silu(conv)
        xs = xc[:, :SSM_WIDTH]
        xs_s[...] = xs
        b_s[...] = xc[:, SSM_WIDTH:SSM_WIDTH + SSM_GROUPS * SSM_STATE]
        ct_s[...] = xc[:, SSM_WIDTH + SSM_GROUPS * SSM_STATE:].T
        dt = dt_ref[...]
        e = e_ref[...]
        xdt_s[...] = (xs * _dot_exact_lhs(dt, e)).T.astype(BF16)
        da = jnp.exp(dt * (-jnp.exp(alog_ref[...])))
        da_s[...] = _dot_exact_lhs(da, e).T
        yt_s[...] = jnp.zeros_like(yt_s)

    n_samp = xs_s.shape[0]
    row = lax.broadcasted_iota(jnp.int32, (n_samp, SSM_STATE), 0)
    lane = lax.broadcasted_iota(jnp.int32, (SSM_STATE, n_samp), 1)
    pick_row = row == s
    decay = _dot_exact_lhs(da_s[...], pick_row.astype(BF16))
    b_row = b_s[pl.ds(s, 1), :]
    for g in range(SSM_GROUPS):
        gs = slice(g * GROUP_WIDTH, (g + 1) * GROUP_WIDTH)
        ns = slice(g * SSM_STATE, (g + 1) * SSM_STATE)
        eb = jnp.where(pick_row, b_row[:, ns], 0.0).astype(BF16)
        upd = _dot(xdt_s[gs, :], eb)
        h_new = st_ref[0, gs, :] * decay[gs, :] + upd
        so_ref[0, gs, :] = h_new
        ce = jnp.where(lane == s, ct_s[ns, :], 0.0).astype(BF16)
        yt_s[gs, :] += _dot(h_new.astype(BF16), ce)

    @pl.when(s == n_s - 1)
    def _():
        y = yt_s[...].T + dsk_ref[...] * xs_s[...]
        y_ref[...] = _gate_and_norm(y, z_ref[...], nw_ref[...]).astype(BF16)


def _ssd_step(xbc, sc, z, dt, cw, cb, alog, dsk, nw, e, state):
    n_s = xbc.shape[0]
    st_spec = pl.BlockSpec((1, SSM_WIDTH, SSM_STATE), lambda s: (s, 0, 0))
    return pl.pallas_call(
        _ssd_step_kernel,
        grid=(n_s,),
        in_specs=[_resident(a.shape) for a in (xbc, sc, z, dt, cw, cb, alog, dsk, nw, e)] + [st_spec],
        out_specs=[_resident((n_s, SSM_WIDTH)), st_spec],
        out_shape=[jax.ShapeDtypeStruct((n_s, SSM_WIDTH), BF16),
                   jax.ShapeDtypeStruct((n_s, SSM_WIDTH, SSM_STATE), F32)],
        scratch_shapes=[pltpu.VMEM((n_s, SSM_WIDTH), F32), pltpu.VMEM((n_s, SSM_GROUPS * SSM_STATE), F32),
                        pltpu.VMEM((SSM_GROUPS * SSM_STATE, n_s), F32), pltpu.VMEM((SSM_WIDTH, n_s), BF16),
                        pltpu.VMEM((SSM_WIDTH, n_s), F32), pltpu.VMEM((SSM_WIDTH, n_s), F32)],
        compiler_params=_cparams("arbitrary"),
        name="ssd_step",
    )(xbc, sc, z, dt, cw, cb, alog, dsk, nw, e, state)


def _rope_q_head(qh, tab, lane):
    r = _rope_fold(qh[:, QK_NOPE:] * tab)
    return qh[:, :QK_NOPE], jnp.where(lane < QK_ROPE, r, 0.0)


def _qkv_kernel(cq_ref, ckv_ref, kpad_ref, tab_ref, wq_ref, wuk_ref, wuv_ref, q_ref, k_ref, v_ref):
    q = _dot(cq_ref[...], wq_ref[...])
    ckv = ckv_ref[...].astype(BF16)
    kn = _dot(ckv, wuk_ref[...])
    v = _dot(ckv, wuv_ref[...])
    tab = tab_ref[...]
    kpad = kpad_ref[...]
    lane = lax.broadcasted_iota(jnp.int32, tab.shape, 1)
    for h in range(MLA_HEADS):
        nope, pe = _rope_q_head(q[:, h * QK_PAD:(h + 1) * QK_PAD], tab, lane)
        q_ref[0, h] = (jnp.concatenate([nope, pe], axis=1) * EXP2_SCALE).astype(BF16)
        k_ref[0, h] = jnp.concatenate([kn[:, h * QK_NOPE:(h + 1) * QK_NOPE].astype(BF16), kpad], axis=1)
        vt = v[:, h * V_HEAD_DIM:(h + 1) * V_HEAD_DIM].T.astype(BF16)
        tkv = v_ref.shape[-1]
        for c in range(v_ref.shape[2]):
            v_ref[0, h, c] = vt[:, c * tkv:(c + 1) * tkv]


def _qkv_prompt(cq, ckv, kpad, tab, wq, wuk, wuv, b, t, tm, tkv):
    nt = t // tm
    nc = tm // tkv
    row = lambda w: pl.BlockSpec((tm, w), lambda bi, i: (bi * nt + i, 0))
    head = lambda w: pl.BlockSpec((1, MLA_HEADS, tm, w), lambda bi, i: (bi, 0, i, 0))
    return pl.pallas_call(
        _qkv_kernel,
        grid=(b, nt),
        in_specs=[row(Q_LORA), row(KV_LORA), row(LANES), pl.BlockSpec((tm, LANES), lambda bi, i: (i, 0)),
                  _resident(wq.shape), _resident(wuk.shape), _resident(wuv.shape)],
        out_specs=[head(QK_PAD), head(QK_PAD),
                   pl.BlockSpec((1, MLA_HEADS, nc, V_HEAD_DIM, tkv), lambda bi, i: (bi, 0, i, 0, 0))],
        out_shape=[jax.ShapeDtypeStruct((b, MLA_HEADS, t, QK_PAD), BF16),
                   jax.ShapeDtypeStruct((b, MLA_HEADS, t, QK_PAD), BF16),
                   jax.ShapeDtypeStruct((b, MLA_HEADS, t // tkv, V_HEAD_DIM, tkv), BF16)],
        compiler_params=_cparams("arbitrary", "arbitrary"),
        name="qkv_prompt",
    )(cq, ckv, kpad, tab, wq, wuk, wuv)


def _q_step_kernel(cq_ref, tab_ref, wq_ref, wukt_ref, ql_ref, qp_ref):
    q = _dot(cq_ref[...], wq_ref[...])
    tab = tab_ref[...]
    lane = lax.broadcasted_iota(jnp.int32, tab.shape, 1)
    for h in range(MLA_HEADS):
        nope, pe = _rope_q_head(q[:, h * QK_PAD:(h + 1) * QK_PAD], tab, lane)
        ql_ref[h] = _dot(nope.astype(BF16), wukt_ref[h]).astype(BF16)
        qp_ref[h] = pe.astype(BF16)


def _q_step(cq, tab, wq, wukt):
    n_s = cq.shape[0]
    return pl.pallas_call(
        _q_step_kernel,
        grid=(1,),
        in_specs=[_resident(a.shape) for a in (cq, tab, wq, wukt)],
        out_specs=[_resident((MLA_HEADS, n_s, KV_LORA)), _resident((MLA_HEADS, n_s, LANES))],
        out_shape=[jax.ShapeDtypeStruct((MLA_HEADS, n_s, KV_LORA), BF16),
                   jax.ShapeDtypeStruct((MLA_HEADS, n_s, LANES), BF16)],
        compiler_params=_cparams("arbitrary"),
        name="q_step",
    )(cq, tab, wq, wukt)


def _softmax_update(s, m_ref, l_ref):
    m_prev = m_ref[...]
    m_new = jnp.maximum(m_prev, jnp.max(s, axis=-1, keepdims=True))
    corr = jnp.exp2((m_prev - m_new) * EXP2_SCALE)
    p = jnp.exp2((s - m_new) * EXP2_SCALE)
    l_ref[...] = l_ref[...] * corr + jnp.sum(p, axis=-1, keepdims=True)
    m_ref[...] = m_new
    return p, corr


def _flash_kernel(q_ref, k_ref, vt_ref, o_ref, *scratch, tq, nh):
    m_s, l_s, acc_s = scratch[:nh], scratch[nh:2 * nh], scratch[2 * nh:]
    qi = pl.program_id(2)
    for h in range(nh):
        m_s[h][...] = jnp.full_like(m_s[h], NEG_BIG)
        l_s[h][...] = jnp.zeros_like(l_s[h])
        acc_s[h][...] = jnp.zeros_like(acc_s[h])

    def step(j, masked):
        start = pl.multiple_of(j * tq, tq)
        sts = [_dot_nt(k_ref[0, h, pl.ds(start, tq), :], q_ref[0, h]) for h in range(nh)]
        ps, corrs = [], []
        for h in range(nh):
            st = sts[h]
            if masked:
                key = lax.broadcasted_iota(jnp.int32, st.shape, 0)
                qry = lax.broadcasted_iota(jnp.int32, st.shape, 1)
                st = jnp.where(key <= qry, st, NEG_BIG)
            m_prev = m_s[h][...]
            m_new = jnp.maximum(m_prev, jnp.max(st, axis=0, keepdims=True))
            corr = jnp.exp2(m_prev - m_new)
            p = jnp.exp2(st - m_new)
            l_s[h][...] = l_s[h][...] * corr + jnp.sum(p, axis=0, keepdims=True)
            m_s[h][...] = m_new
            ps.append(p.astype(BF16))
            corrs.append(corr)
        for h in range(nh):
            acc_s[h][...] = acc_s[h][...] * corrs[h] + _dot(vt_ref[0, h, j], ps[h])

    def body(j, carry):
        step(j, False)
        return carry

    lax.fori_loop(0, qi, body, 0)
    step(qi, True)
    for h in range(nh):
        o_ref[0, :, h * V_HEAD_DIM:(h + 1) * V_HEAD_DIM] = (acc_s[h][...] / l_s[h][...]).T.astype(BF16)


def _flash_prompt(q, k, vt, tq):
    b, h, t, _ = q.shape
    nh = FLASH_HEADS
    return pl.pallas_call(
        functools.partial(_flash_kernel, tq=tq, nh=nh),
        grid=(b, h // nh, t // tq),
        in_specs=[pl.BlockSpec((1, nh, tq, QK_PAD), lambda bi, hi, qi: (bi, hi, qi, 0)),
                  pl.BlockSpec((1, nh, t, QK_PAD), lambda bi, hi, qi: (bi, hi, 0, 0)),
                  pl.BlockSpec((1, nh, t // tq, V_HEAD_DIM, tq), lambda bi, hi, qi: (bi, hi, 0, 0, 0))],
        out_specs=pl.BlockSpec((1, tq, nh * V_HEAD_DIM), lambda bi, hi, qi: (bi, qi, hi)),
        out_shape=jax.ShapeDtypeStruct((b, t, h * V_HEAD_DIM), BF16),
        scratch_shapes=([pltpu.VMEM((1, tq), F32)] * (2 * nh) + [pltpu.VMEM((V_HEAD_DIM, tq), F32)] * nh),
        compiler_params=_cparams("arbitrary", "arbitrary", "arbitrary"),
        name="flash_prompt",
    )(q, k, vt)


def _decode_kernel(pt_ref, ql_ref, qp_ref, cn_ref, kn_ref, ckv_hbm, kpe_hbm, o_ref,
                   kc_buf, kp_buf, sem, m_s, l_s, acc_s, *, n_pg, n_groups, page):
    s = pl.program_id(0)
    n_slots = kc_buf.shape[0]
    ahead = n_slots - 1
    total = pl.num_programs(0) * n_groups

    def copies(samp, grp, slot):
        out = []
        for i in range(n_pg):
            pg = pt_ref[samp, grp * n_pg + i]
            out.append(pltpu.make_async_copy(ckv_hbm.at[pg], kc_buf.at[slot, pl.ds(i * page, page), :],
                                             sem.at[0, slot]))
            out.append(pltpu.make_async_copy(kpe_hbm.at[pg], kp_buf.at[slot, i], sem.at[1, slot]))
        return out

    def start(samp, grp, slot):
        for c in copies(samp, grp, slot):
            c.start()

    @pl.when(s == 0)
    def _():
        for t in range(ahead):
            start(t // n_groups, t % n_groups, t % n_slots)

    ql = ql_ref[0]
    qp = qp_ref[0][:, :QK_ROPE]
    cn = cn_ref[...].astype(BF16).astype(F32)
    kn = kn_ref[...].astype(BF16).astype(F32)
    m_s[...] = (jnp.sum(ql.astype(F32) * cn, axis=-1, keepdims=True)
                + jnp.sum(qp_ref[0].astype(F32) * kn, axis=-1, keepdims=True))
    l_s[...] = jnp.ones_like(l_s)
    acc_s[...] = jnp.broadcast_to(cn, acc_s.shape)

    def group(g, carry):
        t = s * n_groups + g
        slot = lax.rem(t, n_slots)
        g_next = g + ahead
        wraps = g_next >= n_groups

        @pl.when(t + ahead < total)
        def _():
            start(jnp.where(wraps, s + 1, s), jnp.where(wraps, g_next - n_groups, g_next),
                  lax.rem(t + ahead, n_slots))

        for c in copies(s, g, slot):
            c.wait()
        kc = kc_buf[slot].astype(BF16)
        s_pe = [_dot(qp, kp_buf[slot, i].astype(BF16)) for i in range(n_pg)]
        sc = _dot_nt(ql, kc) + jnp.concatenate(s_pe, axis=1)
        p, corr = _softmax_update(sc, m_s, l_s)
        acc_s[...] = acc_s[...] * corr + _dot(p.astype(BF16), kc)
        return carry

    lax.fori_loop(0, n_groups, group, 0)
    o_ref[0] = acc_s[...] / l_s[...]


def _decode_attention(page_table, ql, qp, ckv_new, kpe_new, ckv_pool, kpe_pool_t):
    n_s, n_pages = page_table.shape
    page = ckv_pool.shape[1]
    n_pg = min(DECODE_PAGES, n_pages // 2)
    n_slots = DECODE_SLOTS
    assert n_pages % n_pg == 0 and n_slots - 1 <= n_pages // n_pg
    grid_spec = pltpu.PrefetchScalarGridSpec(
        num_scalar_prefetch=1,
        grid=(n_s,),
        in_specs=[pl.BlockSpec((1, HEAD_PAD, KV_LORA), lambda s, pt: (s, 0, 0)),
                  pl.BlockSpec((1, HEAD_PAD, LANES), lambda s, pt: (s, 0, 0)),
                  pl.BlockSpec((None, 1, KV_LORA), lambda s, pt: (s, 0, 0)),
                  pl.BlockSpec((None, 1, LANES), lambda s, pt: (s, 0, 0)),
                  pl.BlockSpec(memory_space=pl.ANY), pl.BlockSpec(memory_space=pl.ANY)],
        out_specs=pl.BlockSpec((1, HEAD_PAD, KV_LORA), lambda s, pt: (s, 0, 0)),
        scratch_shapes=[pltpu.VMEM((n_slots, n_pg * page, KV_LORA), F32),
                        pltpu.VMEM((n_slots, n_pg, QK_ROPE, page), F32),
                        pltpu.SemaphoreType.DMA((2, n_slots)),
                        pltpu.VMEM((HEAD_PAD, 1), F32), pltpu.VMEM((HEAD_PAD, 1), F32),
                        pltpu.VMEM((HEAD_PAD, KV_LORA), F32)],
    )
    return pl.pallas_call(
        functools.partial(_decode_kernel, n_pg=n_pg, n_groups=n_pages // n_pg, page=page),
        grid_spec=grid_spec,
        out_shape=jax.ShapeDtypeStruct((n_s, HEAD_PAD, KV_LORA), F32),
        compiler_params=_cparams("arbitrary"),
        name="decode_attention",
    )(page_table, ql, qp, ckv_new, kpe_new, ckv_pool, kpe_pool_t)


def _uv_kernel(o_ref, wuv_ref, y_ref):
    for h in range(MLA_HEADS):
        y_ref[:, h * V_HEAD_DIM:(h + 1) * V_HEAD_DIM] = _dot(o_ref[h], wuv_ref[h]).astype(BF16)


def _uv_step(o, wuv):
    n_s = o.shape[1]
    return pl.pallas_call(
        _uv_kernel,
        grid=(1,),
        in_specs=[_resident(o.shape), _resident(wuv.shape)],
        out_specs=_resident((n_s, MLA_HEADS * V_HEAD_DIM)),
        out_shape=jax.ShapeDtypeStruct((n_s, MLA_HEADS * V_HEAD_DIM), BF16),
        compiler_params=_cparams("arbitrary"),
        name="uv_step",
    )(o, wuv)


def _wo_ln_kernel(ys_ref, ym_ref, x_ref, wo_ref, g_ref, b_ref, h_ref):
    mix = _dot(ys_ref[...], wo_ref[0:SSM_WIDTH, :]) + _dot(ym_ref[...], wo_ref[SSM_WIDTH:, :])
    h_ref[...] = _layer_norm(ALPHA * x_ref[...] + mix, g_ref[...], b_ref[...])


def _wo_ln(ys, ym, x, wo, g, b, tm):
    n = x.shape[0]
    row = lambda w: pl.BlockSpec((tm, w), lambda i: (i, 0))
    return pl.pallas_call(
        _wo_ln_kernel,
        grid=(n // tm,),
        in_specs=[row(SSM_WIDTH), row(SSM_WIDTH), row(D_MODEL), _resident(wo.shape), _resident(g.shape),
                  _resident(b.shape)],
        out_specs=row(D_MODEL),
        out_shape=jax.ShapeDtypeStruct((n, D_MODEL), F32),
        compiler_params=_cparams("arbitrary"),
        name="wo_ln",
    )(ys, ym, x, wo, g, b)


def _ffn_tail(j, h_ref, act, wd_ref, g2_ref, b2_ref, y_ref, acc_s):
    acc_s[...] += _dot(act.astype(BF16), wd_ref[...])

    @pl.when(j == pl.num_programs(1) - 1)
    def _():
        y_ref[...] = _layer_norm(ALPHA * h_ref[...] + acc_s[...], g2_ref[...], b2_ref[...])


def _ffn_head(j, h_ref, wg_ref, wu_ref, hb_s, acc_s):
    @pl.when(j == 0)
    def _():
        hb_s[...] = h_ref[...].astype(BF16)
        acc_s[...] = jnp.zeros_like(acc_s)

    hb = hb_s[...]
    return _dot(hb, wg_ref[...]), _dot(hb, wu_ref[...])


def _ffn_prompt_kernel(h_ref, wg_ref, wu_ref, wd_ref, cw_ref, cb_ref, g2_ref, b2_ref, y_ref, gl_ref,
                       hb_s, acc_s, ext_s, carry_s, *, blocks_per_seq):
    i = pl.program_id(0)
    j = pl.program_id(1)
    tm = h_ref.shape[0]
    tf = wg_ref.shape[1]

    @pl.when(j == 0)
    def _():
        hb_s[...] = h_ref[...].astype(BF16)
        acc_s[...] = jnp.zeros_like(acc_s)

    @pl.when(i % blocks_per_seq == 0)
    def _():
        carry_s[j] = jnp.zeros(carry_s.shape[1:], F32)

    hb = hb_s[...]
    w = tf // FFN_PARTS
    parts = [slice(c * w, (c + 1) * w) for c in range(FFN_PARTS)]
    gu = [(_dot(hb, wg_ref[:, cs]), _dot(hb, wu_ref[:, cs])) for cs in parts]
    for cs, (g, u) in zip(parts, gu):
        ext_s[0:SUBLANES, cs] = carry_s[j, :, cs]
        ext_s[SUBLANES:, cs] = g
        gc = cb_ref[:, cs] + cw_ref[FFN_CONV - 1:FFN_CONV, cs] * g
        for k in range(1, FFN_CONV):
            gc = gc + cw_ref[FFN_CONV - 1 - k:FFN_CONV - k, cs] * ext_s[pl.ds(SUBLANES - k, tm), cs]
        tail = ext_s[tm:tm + SUBLANES, cs]
        carry_s[j, :, cs] = tail
        gl_ref[0, :, cs] = tail
        acc_s[...] += _dot((_silu(gc) * u).astype(BF16), wd_ref[cs, :])

    @pl.when(j == pl.num_programs(1) - 1)
    def _():
        y_ref[...] = _layer_norm(ALPHA * h_ref[...] + acc_s[...], g2_ref[...], b2_ref[...])


def _ffn_step_kernel(h_ref, wg_ref, wu_ref, wd_ref, cw_ref, cb_ref, p2_ref, p1_ref, g2_ref, b2_ref,
                     y_ref, gout_ref, hb_s, acc_s):
    j = pl.program_id(1)
    g, u = _ffn_head(j, h_ref, wg_ref, wu_ref, hb_s, acc_s)
    gout_ref[...] = g
    gc = cb_ref[...] + cw_ref[0:1, :] * p2_ref[...] + cw_ref[1:2, :] * p1_ref[...] + cw_ref[2:3, :] * g
    _ffn_tail(j, h_ref, _silu(gc) * u, wd_ref, g2_ref, b2_ref, y_ref, acc_s)


def _ffn_specs(tm, tf):
    nj = D_FF // tf
    return [pl.BlockSpec((tm, D_MODEL), lambda i, j: (i, 0)),
            pl.BlockSpec((D_MODEL, tf), lambda i, j: (0, j)),
            pl.BlockSpec((D_MODEL, tf), lambda i, j: (0, j + nj)),
            pl.BlockSpec((tf, D_MODEL), lambda i, j: (j, 0)),
            pl.BlockSpec((FFN_CONV, tf), lambda i, j: (0, j)),
            pl.BlockSpec((1, tf), lambda i, j: (0, j))]


def _ffn_prompt(h, w_in, w_down, cw, cb, g2, b2, b, t, tm, tf):
    n = h.shape[0]
    nj = D_FF // tf
    bps = t // tm
    vec = pl.BlockSpec((1, D_MODEL), lambda i, j: (0, 0))
    return pl.pallas_call(
        functools.partial(_ffn_prompt_kernel, blocks_per_seq=bps),
        grid=(n // tm, nj),
        in_specs=_ffn_specs(tm, tf) + [vec, vec],
        out_specs=[pl.BlockSpec((tm, D_MODEL), lambda i, j: (i, 0)),
                   pl.BlockSpec((1, SUBLANES, tf), lambda i, j: (i, 0, j))],
        out_shape=[jax.ShapeDtypeStruct((n, D_MODEL), F32), jax.ShapeDtypeStruct((n // tm, SUBLANES, D_FF), F32)],
        scratch_shapes=[pltpu.VMEM((tm, D_MODEL), BF16), pltpu.VMEM((tm, D_MODEL), F32),
                        pltpu.VMEM((tm + SUBLANES, tf), F32), pltpu.VMEM((nj, SUBLANES, tf), F32)],
        compiler_params=_cparams("arbitrary", "arbitrary"),
        name="ffn_prompt",
    )(h, w_in, w_in, w_down, cw, cb, g2, b2)


def _ffn_step(h, w_in, w_down, cw, cb, p2, p1, g2, b2, tf):
    n = h.shape[0]
    nj = D_FF // tf
    vec = pl.BlockSpec((1, D_MODEL), lambda i, j: (0, 0))
    col = pl.BlockSpec((n, tf), lambda i, j: (0, j))
    return pl.pallas_call(
        _ffn_step_kernel,
        grid=(1, nj),
        in_specs=_ffn_specs(n, tf) + [col, col, vec, vec],
        out_specs=[pl.BlockSpec((n, D_MODEL), lambda i, j: (0, 0)), col],
        out_shape=[jax.ShapeDtypeStruct((n, D_MODEL), F32), jax.ShapeDtypeStruct((n, D_FF), F32)],
        scratch_shapes=[pltpu.VMEM((n, D_MODEL), BF16), pltpu.VMEM((n, D_MODEL), F32)],
        compiler_params=_cparams("arbitrary", "arbitrary"),
        name="ffn_step",
    )(h, w_in, w_in, w_down, cw, cb, p2, p1, g2, b2)


def _rope_table(pos):
    inv_freq = ROPE_THETA ** (-jnp.arange(0, QK_ROPE, 2, dtype=F32) / QK_ROPE)
    ang = pos.astype(F32)[:, None] * inv_freq[None, :]
    c, s = jnp.cos(ang), jnp.sin(ang)
    return jnp.concatenate([c, c, -s, s], axis=1)


def _swap_halves(w):
    half = w.shape[-1] // 2
    return jnp.concatenate([w[..., half:], w[..., :half]], axis=-1)


def _pad_lanes(v):
    return jnp.pad(v.reshape(1, -1).astype(F32), ((0, 0), (0, LANES - v.shape[-1])))


def _tile(n, cap):
    t = min(n, cap)
    assert n % t == 0
    return t


def kernel(x_prompt, x_sample, cache_ckv, cache_kpe, page_table, state_ssm, state_conv, state_ffn_conv,
           w_in, conv_w, conv_b, dt_bias, a_log, d_skip, ssm_norm_w, q_norm_w, kv_norm_w,
           w_uq, w_uk, w_uv, w_o, ln1_g, ln1_b, w_ffn_in, ffn_conv_w, ffn_conv_b, w_ffn_down, ln2_g, ln2_b):
    assert w_in.shape[0] == DEPTH == 1 and x_sample.shape[1] == 1
    b, t, _ = x_prompt.shape
    n_s = x_sample.shape[0]
    n_pages = page_table.shape[1]
    past_len = n_pages * cache_ckv.shape[2]
    assert t % SSM_CHUNK == 0

    wi = w_in[0]
    w_kpe = wi[:, OFF_CKV:]
    w1 = jnp.concatenate([wi[:, :OFF_XBC], wi[:, OFF_DT:OFF_CKV], w_kpe, _swap_halves(w_kpe),
                          wi[:, OFF_XBC:OFF_DT], jnp.zeros((D_MODEL, LANES - SSM_HEADS), F32)], axis=1).astype(BF16)
    uq = w_uq[0]
    uq_pe = uq[:, :, QK_NOPE:]
    wq = jnp.concatenate([uq[:, :, :QK_NOPE], uq_pe, _swap_halves(uq_pe)], axis=-1)
    wq = wq.reshape(Q_LORA, MLA_HEADS * QK_PAD).astype(BF16)
    wuk = w_uk[0].reshape(KV_LORA, MLA_HEADS * QK_NOPE).astype(BF16)
    wuv = w_uv[0].reshape(KV_LORA, MLA_HEADS * V_HEAD_DIM).astype(BF16)
    wukt = jnp.transpose(w_uk[0], (1, 2, 0)).astype(BF16)
    wuv_h = jnp.transpose(w_uv[0], (1, 0, 2)).astype(BF16)
    wo = w_o[0].astype(BF16)
    wf_in = w_ffn_in[0].astype(BF16)
    wf_down = w_ffn_down[0].astype(BF16)
    row = lambda v: v.reshape(1, -1).astype(F32)
    dtb, alog = _pad_lanes(dt_bias[0]), _pad_lanes(a_log[0])
    dsk = row(jnp.repeat(d_skip[0], SSM_HEAD_DIM))
    nw, qg, kg = row(ssm_norm_w[0]), row(q_norm_w[0]), row(kv_norm_w[0])
    cw, cb = conv_w[0], row(conv_b[0])
    fcw, fcb = ffn_conv_w[0], row(ffn_conv_b[0])
    g1, b1, g2, b2 = row(ln1_g[0]), row(ln1_b[0]), row(ln2_g[0]), row(ln2_b[0])
    expand = (jnp.arange(SSM_WIDTH)[None, :] // SSM_HEAD_DIM == jnp.arange(LANES)[:, None]).astype(BF16)
    tril = (jnp.arange(SSM_CHUNK)[:, None] >= jnp.arange(SSM_CHUNK)[None, :]).astype(BF16)
    tab_p = _rope_table(jnp.arange(t))
    tab_s = _rope_table(jnp.full((n_s,), past_len))

    tm = _tile(t, ROW_TILE)
    xp = x_prompt.reshape(b * t, D_MODEL)
    z, xbc, cq, ckv, kpe, kpad, dt = _in_proj(xp, w1, tab_p, dtb, qg, kg, tm)
    y_ssd, h_fin = _ssd_prompt(xbc, z, dt, cw, cb, alog, dsk, nw, expand, tril, b, t)
    tq = _tile(tm, FLASH_TILE)
    q, k, vt = _qkv_prompt(cq, ckv, kpad, tab_p, wq, wuk, wuv, b, t, tm, tq)
    y_mla = _flash_prompt(q, k, vt, tq).reshape(b * t, MLA_HEADS * V_HEAD_DIM)
    h1 = _wo_ln(y_ssd, y_mla, xp, wo, g1, b1, tm)
    tf = FF_TILE
    y_p, g_last = _ffn_prompt(h1, wf_in, wf_down, fcw, fcb, g2, b2, b, t, tm, tf)

    xs_ = x_sample.reshape(n_s, D_MODEL)
    z_s, xbc_s, cq_s, ckv_s, kpe_s, kpad_s, dt_s = _in_proj(xs_, w1, tab_s, dtb, qg, kg, n_s)
    sc = jnp.transpose(state_conv[0], (1, 0, 2))
    y_ssd_s, st_new = _ssd_step(xbc_s, sc, z_s, dt_s, cw, cb, alog, dsk, nw, expand,
                                state_ssm[0].reshape(n_s, SSM_WIDTH, SSM_STATE))
    ql, qp = _q_step(cq_s, tab_s, wq, wukt)
    pad_heads = lambda a: jnp.pad(jnp.transpose(a, (1, 0, 2)), ((0, 0), (0, HEAD_PAD - MLA_HEADS), (0, 0)))
    o_lat = _decode_attention(page_table, pad_heads(ql), pad_heads(qp), ckv_s.reshape(n_s, 1, KV_LORA),
                              kpad_s.astype(F32).reshape(n_s, 1, LANES), cache_ckv[0],
                              jnp.swapaxes(cache_kpe[0], 1, 2))
    y_mla_s = _uv_step(jnp.transpose(o_lat[:, :MLA_HEADS], (1, 0, 2)).astype(BF16), wuv_h)
    h1_s = _wo_ln(y_ssd_s, y_mla_s, xs_, wo, g1, b1, n_s)
    fbuf = state_ffn_conv[0]
    y_s, g_s = _ffn_step(h1_s, wf_in, wf_down, fcw, fcb, fbuf[:, 0], fbuf[:, 1], g2, b2, tf)

    lead = lambda a: a[None]
    return (y_p.reshape(b, t, D_MODEL),
            y_s.reshape(n_s, 1, D_MODEL),
            lead(ckv.reshape(b, t, KV_LORA)),
            lead(kpe.reshape(b, t, QK_ROPE)),
            lead(h_fin.reshape(b, SSM_HEADS, SSM_HEAD_DIM, SSM_STATE)),
            lead(xbc.reshape(b, t, CONV_DIM)[:, t - (SSM_CONV - 1):]),
            lead(g_last.reshape(b, t // tm, SUBLANES, D_FF)[:, -1, SUBLANES - (FFN_CONV - 1):]),
            lead(ckv_s.reshape(n_s, 1, KV_LORA)),
            lead(kpe_s.reshape(n_s, 1, QK_ROPE)),
            lead(st_new.reshape(n_s, SSM_HEADS, SSM_HEAD_DIM, SSM_STATE)),
            lead(jnp.concatenate([state_conv[0][:, 1:], xbc_s[:, None]], axis=1)),
            lead(jnp.concatenate([fbuf[:, 1:], g_s[:, None]], axis=1)))
```

```python
import functools
import math

import jax
import jax.numpy as jnp
from jax import lax
from jax.experimental import pallas as pl
from jax.experimental.pallas import tpu as pltpu

F32 = jnp.float32
BF16 = jnp.bfloat16

D_MODEL = 2048
SSM_WIDTH = 1024
SSM_HEAD_DIM = 64
SSM_HEADS = 16
SSM_GROUPS = 2
SSM_STATE = 128
SSM_CONV = 4
SSM_CHUNK = 128
CONV_DIM = SSM_WIDTH + 2 * SSM_GROUPS * SSM_STATE
GROUP_WIDTH = SSM_WIDTH // SSM_GROUPS
MLA_HEADS = 8
V_HEAD_DIM = 128
QK_NOPE = 128
QK_ROPE = 64
Q_LORA = 512
KV_LORA = 512
ROPE_THETA = 10000.0
ATTN_SCALE = (QK_NOPE + QK_ROPE) ** -0.5
D_FF = 5632
FFN_CONV = 3
LN_EPS = 1e-5
RMS_EPS = 1e-6
DEPTH = 1
ALPHA = (2.0 * DEPTH) ** 0.25
OFF_Z = SSM_WIDTH
OFF_XBC = OFF_Z + CONV_DIM
OFF_DT = OFF_XBC + SSM_HEADS
OFF_CQ = OFF_DT + Q_LORA
OFF_CKV = OFF_CQ + KV_LORA

LANES = 128
SUBLANES = 8
QK_PAD = 2 * LANES
HEAD_PAD = 16
NEG_BIG = -1e30
VMEM_LIMIT = 56 * 1024 * 1024
ROW_TILE = 512
FF_TILE = 512
FFN_PARTS = 2
WO_PARTS = 2
SSD_STEP_SAMPLES = 4
FLASH_HEADS = 4
FLASH_TILE = 512
DECODE_PAGES = 16
DECODE_SLOTS = 3
EXP2_SCALE = ATTN_SCALE * math.log2(math.e)


def _cparams(*sem):
    return pltpu.CompilerParams(dimension_semantics=sem, vmem_limit_bytes=VMEM_LIMIT)


def _dot(a, b):
    return jnp.dot(a, b, preferred_element_type=F32)


def _dot_nt(a, b):
    return lax.dot_general(a, b, (((1,), (1,)), ((), ())), preferred_element_type=F32)


def _split3(x):
    hi = x.astype(BF16)
    r = x - hi.astype(F32)
    mid = r.astype(BF16)
    lo = (r - mid.astype(F32)).astype(BF16)
    return hi, mid, lo


def _dot_exact_lhs(x, sel):
    hi, mid, lo = _split3(x)
    return _dot(hi, sel) + _dot(mid, sel) + _dot(lo, sel)


def _dot_exact_rhs(sel, x):
    hi, mid, lo = _split3(x)
    return _dot(sel, hi) + _dot(sel, mid) + _dot(sel, lo)


def _silu(x):
    hx = 0.5 * x
    return hx + hx * jnp.tanh(hx)


def _softplus(x):
    return jnp.maximum(x, 0.0) + jnp.log1p(jnp.exp(-jnp.abs(x)))


def _rms(x, g):
    r = lax.rsqrt(jnp.mean(x * x, axis=-1, keepdims=True) + RMS_EPS)
    return x * r * g


def _layer_norm(v, g, b):
    mu = jnp.mean(v, axis=-1, keepdims=True)
    d = v - mu
    var = jnp.mean(d * d, axis=-1, keepdims=True)
    return d * lax.rsqrt(var + LN_EPS) * g + b


def _rope_fold(t):
    return t + pltpu.roll(t, QK_ROPE, 1)


def _resident(shape):
    nd = len(shape)
    return pl.BlockSpec(shape, lambda *_: (0,) * nd)


def _in_proj_kernel(x_ref, wssd_ref, wlat_ref, wmisc_ref, tab_ref, dtb_ref, qg_ref, kg_ref,
                    z_ref, xbc_ref, cq_ref, ckv_ref, kpe_ref, kpad_ref, dt_ref):
    xb = x_ref[...].astype(BF16)
    z_ref[...] = _dot(xb, wssd_ref[:, :SSM_WIDTH])
    xbc_ref[...] = _dot(xb, wssd_ref[:, SSM_WIDTH:])
    cq_ref[...] = _rms(_dot(xb, wlat_ref[:, :Q_LORA]), qg_ref[...]).astype(BF16)
    ckv_ref[...] = _rms(_dot(xb, wlat_ref[:, Q_LORA:]), kg_ref[...])
    kr = _rope_fold(_dot(xb, wmisc_ref[:, :LANES]) * tab_ref[...])
    kpe_ref[...] = kr[:, :QK_ROPE]
    lane = lax.broadcasted_iota(jnp.int32, kr.shape, 1)
    kpad_ref[...] = jnp.where(lane < QK_ROPE, kr, 0.0).astype(BF16)
    dt_ref[...] = _softplus(_dot(xb, wmisc_ref[:, LANES:]) + dtb_ref[...])


def _in_proj(x, w1, tab, dtb, qg, kg, tm):
    n = x.shape[0]
    n_tab = tab.shape[0] // tm
    row = lambda w: pl.BlockSpec((tm, w), lambda i: (i, 0))
    return pl.pallas_call(
        _in_proj_kernel,
        grid=(n // tm,),
        in_specs=[row(D_MODEL)] + [_resident(w.shape) for w in w1] + [
                  pl.BlockSpec((tm, LANES), lambda i: (i % n_tab, 0)),
                  _resident(dtb.shape), _resident(qg.shape), _resident(kg.shape)],
        out_specs=[row(SSM_WIDTH), row(CONV_DIM), row(Q_LORA), row(KV_LORA), row(QK_ROPE), row(LANES),
                   row(LANES)],
        out_shape=[jax.ShapeDtypeStruct((n, SSM_WIDTH), F32), jax.ShapeDtypeStruct((n, CONV_DIM), F32),
                   jax.ShapeDtypeStruct((n, Q_LORA), BF16), jax.ShapeDtypeStruct((n, KV_LORA), F32),
                   jax.ShapeDtypeStruct((n, QK_ROPE), F32), jax.ShapeDtypeStruct((n, LANES), BF16),
                   jax.ShapeDtypeStruct((n, LANES), F32)],
        compiler_params=_cparams("arbitrary"),
        name="in_proj",
    )(x, *w1, tab, dtb, qg, kg)


def _gate_and_norm(y, z, nw):
    yg = y * _silu(z)
    parts = []
    for g in range(SSM_GROUPS):
        v = yg[:, g * GROUP_WIDTH:(g + 1) * GROUP_WIDTH]
        parts.append(v * lax.rsqrt(jnp.mean(v * v, axis=-1, keepdims=True) + RMS_EPS))
    return jnp.concatenate(parts, axis=1) * nw


def _ssd_kernel(xbc_ref, z_ref, dt_ref, cw_ref, cb_ref, alog_ref, dsk_ref, nw_ref, e_ref, tril_ref,
                y_ref, hout_ref, ext_ref, ht_ref):
    c = pl.program_id(1)
    L = SSM_CHUNK

    @pl.when(c == 0)
    def _():
        ext_ref[0:SUBLANES, :] = jnp.zeros((SUBLANES, CONV_DIM), F32)
        ht_ref[...] = jnp.zeros_like(ht_ref)

    ext_ref[SUBLANES:SUBLANES + L, :] = xbc_ref[...]
    conv = cb_ref[...] + cw_ref[SSM_CONV - 1:SSM_CONV, :] * xbc_ref[...]
    for k in range(1, SSM_CONV):
        conv = conv + cw_ref[SSM_CONV - 1 - k:SSM_CONV - k, :] * ext_ref[pl.ds(SUBLANES - k, L), :]
    ext_ref[0:SUBLANES, :] = ext_ref[L:L + SUBLANES, :]
    xc = _silu(conv)
    xs = xc[:, :SSM_WIDTH]
    bm = xc[:, SSM_WIDTH:SSM_WIDTH + SSM_GROUPS * SSM_STATE]
    cm = xc[:, SSM_WIDTH + SSM_GROUPS * SSM_STATE:]

    dt = dt_ref[...]
    da = dt * (-jnp.exp(alog_ref[...]))
    acs = _dot_exact_rhs(tril_ref[...], da)
    e = e_ref[...]
    acs_x = _dot_exact_lhs(acs, e)
    dt_x = _dot_exact_lhs(dt, e)
    last_x = acs_x[L - 1:L, :]
    xw = xs * dt_x * jnp.exp(last_x - acs_x)
    exp_acs = jnp.exp(acs_x)
    acs_t = acs.T
    dt_t = dt.T

    row = lax.broadcasted_iota(jnp.int32, (L, L), 0)
    col = lax.broadcasted_iota(jnp.int32, (L, L), 1)
    causal = row >= col
    lower_half = col < SSM_HEAD_DIM

    hprev = ht_ref[...].astype(BF16)
    y_parts = []
    heads_per_group = SSM_HEADS // SSM_GROUPS
    for g in range(SSM_GROUPS):
        gs = slice(g * GROUP_WIDTH, (g + 1) * GROUP_WIDTH)
        bg = bm[:, g * SSM_STATE:(g + 1) * SSM_STATE]
        cg = cm[:, g * SSM_STATE:(g + 1) * SSM_STATE].astype(BF16)
        cb = _dot_nt(cg, bg.astype(BF16))
        y_off = _dot(cg, hprev[:, gs]) * exp_acs[:, gs]
        for jj in range(heads_per_group // 2):
            j = g * (heads_per_group // 2) + jj
            xp = xs[:, j * LANES:(j + 1) * LANES]
            yp = None
            for h, xh in ((2 * j, jnp.where(lower_half, xp, 0.0)), (2 * j + 1, jnp.where(lower_half, 0.0, xp))):
                diff = acs[:, h:h + 1] - acs_t[h:h + 1, :]
                dec = jnp.exp(jnp.where(causal, diff, NEG_BIG))
                m = (cb * dec * dt_t[h:h + 1, :]).astype(BF16)
                t = _dot(m, xh.astype(BF16))
                yp = t if yp is None else yp + t
            y_parts.append(yp + y_off[:, jj * LANES:(jj + 1) * LANES])
        ht_ref[:, gs] = ht_ref[:, gs] * jnp.exp(last_x[:, gs]) + _dot(bg.T.astype(BF16), xw[:, gs].astype(BF16))

    y = jnp.concatenate(y_parts, axis=1) + dsk_ref[...] * xs
    y_ref[...] = _gate_and_norm(y, z_ref[...], nw_ref[...]).astype(BF16)

    @pl.when(c == pl.num_programs(1) - 1)
    def _():
        hout_ref[0] = ht_ref[...].T


def _ssd_prompt(xbc, z, dt, cw, cb, alog, dsk, nw, e, tril, b, t):
    nc = t // SSM_CHUNK
    L = SSM_CHUNK
    row = lambda w: pl.BlockSpec((L, w), lambda bi, c: (bi * nc + c, 0))
    return pl.pallas_call(
        _ssd_kernel,
        grid=(b, nc),
        in_specs=[row(CONV_DIM), row(SSM_WIDTH), row(LANES)] +
                 [_resident(a.shape) for a in (cw, cb, alog, dsk, nw, e, tril)],
        out_specs=[row(SSM_WIDTH), pl.BlockSpec((1, SSM_WIDTH, SSM_STATE), lambda bi, c: (bi, 0, 0))],
        out_shape=[jax.ShapeDtypeStruct((b * t, SSM_WIDTH), BF16),
                   jax.ShapeDtypeStruct((b, SSM_WIDTH, SSM_STATE), F32)],
        scratch_shapes=[pltpu.VMEM((L + SUBLANES, CONV_DIM), F32), pltpu.VMEM((SSM_STATE, SSM_WIDTH), F32)],
        compiler_params=_cparams("arbitrary", "arbitrary"),
        name="ssd_prompt",
    )(xbc, z, dt, cw, cb, alog, dsk, nw, e, tril)


def _ssd_step_kernel(xbc_ref, sc_ref, z_ref, dt_ref, cw_ref, cb_ref, alog_ref, dsk_ref, nw_ref, e_ref, st_ref,
                     y_ref, so_ref, xs_s, b_s, ct_s, xdt_s, da_s, yt_s):
    blk = pl.program_id(0)
    per_step = st_ref.shape[0]

    @pl.when(blk == 0)
    def _():
        conv = cb_ref[...] + cw_ref[SSM_CONV - 1:SSM_CONV, :] * xbc_ref[...]
        for k in range(SSM_CONV - 1):
            conv = conv + cw_ref[k:k + 1, :] * sc_ref[k]
        xc = _silu(conv)
        xs = xc[:, :SSM_WIDTH]
        xs_s[...] = xs
        b_s[...] = xc[:, SSM_WIDTH:SSM_WIDTH + SSM_GROUPS * SSM_STATE]
        ct_s[...] = xc[:, SSM_WIDTH + SSM_GROUPS * SSM_STATE:].T
        dt = dt_ref[...]
        xdt_s[...] = (xs * _dot_exact_lhs(dt, e_ref[...])).T.astype(BF16)
        da_s[...] = jnp.exp(dt * (-jnp.exp(alog_ref[...])))
        yt_s[...] = jnp.zeros_like(yt_s)

    n_samp = xs_s.shape[0]
    row = lax.broadcasted_iota(jnp.int32, (n_samp, SSM_STATE), 0)
    lane = lax.broadcasted_iota(jnp.int32, (SSM_STATE, n_samp), 1)
    heads_per_group = SSM_HEADS // SSM_GROUPS
    for k in range(per_step):
        s = blk * per_step + k
        pick_row = row == s
        decay = jnp.broadcast_to(da_s[pl.ds(s, 1), :], (SSM_STATE, SSM_STATE)).T
        b_row = b_s[pl.ds(s, 1), :]
        for g in range(SSM_GROUPS):
            gs = slice(g * GROUP_WIDTH, (g + 1) * GROUP_WIDTH)
            ns = slice(g * SSM_STATE, (g + 1) * SSM_STATE)
            eb = jnp.where(pick_row, b_row[:, ns], 0.0).astype(BF16)
            upd = _dot(xdt_s[gs, :], eb)
            for hh in range(heads_per_group):
                h = g * heads_per_group + hh
                rs = slice(h * SSM_HEAD_DIM, (h + 1) * SSM_HEAD_DIM)
                so_ref[k, rs, :] = (st_ref[k, rs, :] * decay[h:h + 1, :]
                                    + upd[hh * SSM_HEAD_DIM:(hh + 1) * SSM_HEAD_DIM, :])
            ce = jnp.where(lane == s, ct_s[ns, :], 0.0).astype(BF16)
            yt_s[gs, :] += _dot(so_ref[k, gs, :].astype(BF16), ce)

    @pl.when(blk == pl.num_programs(0) - 1)
    def _():
        y = yt_s[...].T + dsk_ref[...] * xs_s[...]
        y_ref[...] = _gate_and_norm(y, z_ref[...], nw_ref[...]).astype(BF16)


def _ssd_step(xbc, sc, z, dt, cw, cb, alog, dsk, nw, e, state):
    n_s = xbc.shape[0]
    per_step = _tile(n_s, SSD_STEP_SAMPLES)
    st_spec = pl.BlockSpec((per_step, SSM_WIDTH, SSM_STATE), lambda s: (s, 0, 0))
    return pl.pallas_call(
        _ssd_step_kernel,
        grid=(n_s // per_step,),
        in_specs=[_resident(a.shape) for a in (xbc, sc, z, dt, cw, cb, alog, dsk, nw, e)] + [st_spec],
        out_specs=[_resident((n_s, SSM_WIDTH)), st_spec],
        out_shape=[jax.ShapeDtypeStruct((n_s, SSM_WIDTH), BF16),
                   jax.ShapeDtypeStruct((n_s, SSM_WIDTH, SSM_STATE), F32)],
        scratch_shapes=[pltpu.VMEM((n_s, SSM_WIDTH), F32), pltpu.VMEM((n_s, SSM_GROUPS * SSM_STATE), F32),
                        pltpu.VMEM((SSM_GROUPS * SSM_STATE, n_s), F32), pltpu.VMEM((SSM_WIDTH, n_s), BF16),
                        pltpu.VMEM((n_s, LANES), F32), pltpu.VMEM((SSM_WIDTH, n_s), F32)],
        compiler_params=_cparams("arbitrary"),
        name="ssd_step",
    )(xbc, sc, z, dt, cw, cb, alog, dsk, nw, e, state)


def _rope_q_head(qh, tab, lane):
    r = _rope_fold(qh[:, QK_NOPE:] * tab)
    return qh[:, :QK_NOPE], jnp.where(lane < QK_ROPE, r, 0.0)


def _qkv_kernel(cq_ref, ckv_ref, kpad_ref, tab_ref, wq_ref, wuk_ref, wuv_ref, q_ref, k_ref, v_ref):
    q = _dot(cq_ref[...], wq_ref[...])
    ckv = ckv_ref[...].astype(BF16)
    kn = _dot(ckv, wuk_ref[...])
    v = _dot(ckv, wuv_ref[...])
    tab = tab_ref[...]
    kpad = kpad_ref[...]
    lane = lax.broadcasted_iota(jnp.int32, tab.shape, 1)
    for h in range(MLA_HEADS):
        nope, pe = _rope_q_head(q[:, h * QK_PAD:(h + 1) * QK_PAD], tab, lane)
        q_ref[0, h] = (jnp.concatenate([nope, pe], axis=1) * EXP2_SCALE).astype(BF16)
        k_ref[0, h] = jnp.concatenate([kn[:, h * QK_NOPE:(h + 1) * QK_NOPE].astype(BF16), kpad], axis=1)
        vt = v[:, h * V_HEAD_DIM:(h + 1) * V_HEAD_DIM].T.astype(BF16)
        tkv = v_ref.shape[-1]
        for c in range(v_ref.shape[2]):
            v_ref[0, h, c] = vt[:, c * tkv:(c + 1) * tkv]


def _qkv_prompt(cq, ckv, kpad, tab, wq, wuk, wuv, b, t, tm, tkv):
    nt = t // tm
    nc = tm // tkv
    row = lambda w: pl.BlockSpec((tm, w), lambda bi, i: (bi * nt + i, 0))
    head = lambda w: pl.BlockSpec((1, MLA_HEADS, tm, w), lambda bi, i: (bi, 0, i, 0))
    return pl.pallas_call(
        _qkv_kernel,
        grid=(b, nt),
        in_specs=[row(Q_LORA), row(KV_LORA), row(LANES), pl.BlockSpec((tm, LANES), lambda bi, i: (i, 0)),
                  _resident(wq.shape), _resident(wuk.shape), _resident(wuv.shape)],
        out_specs=[head(QK_PAD), head(QK_PAD),
                   pl.BlockSpec((1, MLA_HEADS, nc, V_HEAD_DIM, tkv), lambda bi, i: (bi, 0, i, 0, 0))],
        out_shape=[jax.ShapeDtypeStruct((b, MLA_HEADS, t, QK_PAD), BF16),
                   jax.ShapeDtypeStruct((b, MLA_HEADS, t, QK_PAD), BF16),
                   jax.ShapeDtypeStruct((b, MLA_HEADS, t // tkv, V_HEAD_DIM, tkv), BF16)],
        compiler_params=_cparams("arbitrary", "arbitrary"),
        name="qkv_prompt",
    )(cq, ckv, kpad, tab, wq, wuk, wuv)


def _q_step_kernel(cq_ref, tab_ref, wq_ref, wukt_ref, ql_ref, qp_ref):
    q = _dot(cq_ref[...], wq_ref[...])
    tab = tab_ref[...]
    lane = lax.broadcasted_iota(jnp.int32, tab.shape, 1)
    for h in range(MLA_HEADS):
        nope, pe = _rope_q_head(q[:, h * QK_PAD:(h + 1) * QK_PAD], tab, lane)
        ql_ref[h] = _dot(nope.astype(BF16), wukt_ref[h]).astype(BF16)
        qp_ref[h] = pe.astype(BF16)


def _q_step(cq, tab, wq, wukt):
    n_s = cq.shape[0]
    return pl.pallas_call(
        _q_step_kernel,
        grid=(1,),
        in_specs=[_resident(a.shape) for a in (cq, tab, wq, wukt)],
        out_specs=[_resident((MLA_HEADS, n_s, KV_LORA)), _resident((MLA_HEADS, n_s, LANES))],
        out_shape=[jax.ShapeDtypeStruct((MLA_HEADS, n_s, KV_LORA), BF16),
                   jax.ShapeDtypeStruct((MLA_HEADS, n_s, LANES), BF16)],
        compiler_params=_cparams("arbitrary"),
        name="q_step",
    )(cq, tab, wq, wukt)


def _softmax_update(s, m_ref, l_ref):
    m_prev = m_ref[...]
    m_new = jnp.maximum(m_prev, jnp.max(s, axis=-1, keepdims=True))
    corr = jnp.exp2((m_prev - m_new) * EXP2_SCALE)
    p = jnp.exp2((s - m_new) * EXP2_SCALE)
    l_ref[...] = l_ref[...] * corr + jnp.sum(p, axis=-1, keepdims=True)
    m_ref[...] = m_new
    return p, corr


def _flash_kernel(q_ref, k_ref, vt_ref, o_ref, *scratch, tq, nh):
    m_s, l_s, acc_s = scratch[:nh], scratch[nh:2 * nh], scratch[2 * nh:]
    qi = pl.program_id(2)
    for h in range(nh):
        m_s[h][...] = jnp.full_like(m_s[h], NEG_BIG)
        l_s[h][...] = jnp.zeros_like(l_s[h])
        acc_s[h][...] = jnp.zeros_like(acc_s[h])

    def step(j, masked):
        start = pl.multiple_of(j * tq, tq)
        sts = [_dot_nt(k_ref[0, h, pl.ds(start, tq), :], q_ref[0, h]) for h in range(nh)]
        ps, corrs = [], []
        for h in range(nh):
            st = sts[h]
            if masked:
                key = lax.broadcasted_iota(jnp.int32, st.shape, 0)
                qry = lax.broadcasted_iota(jnp.int32, st.shape, 1)
                st = jnp.where(key <= qry, st, NEG_BIG)
            m_prev = m_s[h][...]
            m_new = jnp.maximum(m_prev, jnp.max(st, axis=0, keepdims=True))
            corr = jnp.exp2(m_prev - m_new)
            p = jnp.exp2(st - m_new)
            l_s[h][...] = l_s[h][...] * corr + jnp.sum(p, axis=0, keepdims=True)
            m_s[h][...] = m_new
            ps.append(p.astype(BF16))
            corrs.append(corr)
        for h in range(nh):
            acc_s[h][...] = acc_s[h][...] * corrs[h] + _dot(vt_ref[0, h, j], ps[h])

    def body(j, carry):
        step(j, False)
        return carry

    lax.fori_loop(0, qi, body, 0)
    step(qi, True)
    for h in range(nh):
        o_ref[0, :, h * V_HEAD_DIM:(h + 1) * V_HEAD_DIM] = (acc_s[h][...] / l_s[h][...]).T.astype(BF16)


def _flash_prompt(q, k, vt, tq):
    b, h, t, _ = q.shape
    nh = FLASH_HEADS
    return pl.pallas_call(
        functools.partial(_flash_kernel, tq=tq, nh=nh),
        grid=(b, h // nh, t // tq),
        in_specs=[pl.BlockSpec((1, nh, tq, QK_PAD), lambda bi, hi, qi: (bi, hi, qi, 0)),
                  pl.BlockSpec((1, nh, t, QK_PAD), lambda bi, hi, qi: (bi, hi, 0, 0)),
                  pl.BlockSpec((1, nh, t // tq, V_HEAD_DIM, tq), lambda bi, hi, qi: (bi, hi, 0, 0, 0))],
        out_specs=pl.BlockSpec((1, tq, nh * V_HEAD_DIM), lambda bi, hi, qi: (bi, qi, hi)),
        out_shape=jax.ShapeDtypeStruct((b, t, h * V_HEAD_DIM), BF16),
        scratch_shapes=([pltpu.VMEM((1, tq), F32)] * (2 * nh) + [pltpu.VMEM((V_HEAD_DIM, tq), F32)] * nh),
        compiler_params=_cparams("arbitrary", "arbitrary", "arbitrary"),
        name="flash_prompt",
    )(q, k, vt)


def _decode_kernel(pt_ref, ql_ref, qp_ref, cn_ref, kn_ref, ckv_hbm, kpe_hbm, o_ref,
                   kc_buf, kp_buf, sem, m_s, l_s, acc_s, *, n_pg, n_groups, page):
    s = pl.program_id(0)
    n_slots = kc_buf.shape[0]
    ahead = n_slots - 1
    total = pl.num_programs(0) * n_groups

    def copies(samp, grp, slot):
        out = []
        for i in range(n_pg):
            pg = pt_ref[samp, grp * n_pg + i]
            out.append(pltpu.make_async_copy(ckv_hbm.at[pg], kc_buf.at[slot, pl.ds(i * page, page), :],
                                             sem.at[0, slot]))
            out.append(pltpu.make_async_copy(kpe_hbm.at[pg], kp_buf.at[slot, i], sem.at[1, slot]))
        return out

    def start(samp, grp, slot):
        for c in copies(samp, grp, slot):
            c.start()

    @pl.when(s == 0)
    def _():
        for t in range(ahead):
            start(t // n_groups, t % n_groups, t % n_slots)

    ql = ql_ref[0]
    qp = qp_ref[0][:, :QK_ROPE]
    cn = cn_ref[...].astype(BF16).astype(F32)
    kn = kn_ref[...].astype(BF16).astype(F32)
    m_s[...] = (jnp.sum(ql.astype(F32) * cn, axis=-1, keepdims=True)
                + jnp.sum(qp_ref[0].astype(F32) * kn, axis=-1, keepdims=True))
    l_s[...] = jnp.ones_like(l_s)
    acc_s[...] = jnp.broadcast_to(cn, acc_s.shape)

    def group(g, carry):
        t = s * n_groups + g
        slot = lax.rem(t, n_slots)
        g_next = g + ahead
        wraps = g_next >= n_groups

        @pl.when(t + ahead < total)
        def _():
            start(jnp.where(wraps, s + 1, s), jnp.where(wraps, g_next - n_groups, g_next),
                  lax.rem(t + ahead, n_slots))

        for c in copies(s, g, slot):
            c.wait()
        kc = kc_buf[slot].astype(BF16)
        s_pe = [_dot(qp, kp_buf[slot, i].astype(BF16)) for i in range(n_pg)]
        sc = _dot_nt(ql, kc) + jnp.concatenate(s_pe, axis=1)
        p, corr = _softmax_update(sc, m_s, l_s)
        acc_s[...] = acc_s[...] * corr + _dot(p.astype(BF16), kc)
        return carry

    lax.fori_loop(0, n_groups, group, 0)
    o_ref[0] = acc_s[...] / l_s[...]


def _decode_attention(page_table, ql, qp, ckv_new, kpe_new, ckv_pool, kpe_pool_t):
    n_s, n_pages = page_table.shape
    page = ckv_pool.shape[1]
    n_pg = min(DECODE_PAGES, n_pages // 2)
    n_slots = DECODE_SLOTS
    assert n_pages % n_pg == 0 and n_slots - 1 <= n_pages // n_pg
    grid_spec = pltpu.PrefetchScalarGridSpec(
        num_scalar_prefetch=1,
        grid=(n_s,),
        in_specs=[pl.BlockSpec((1, HEAD_PAD, KV_LORA), lambda s, pt: (s, 0, 0)),
                  pl.BlockSpec((1, HEAD_PAD, LANES), lambda s, pt: (s, 0, 0)),
                  pl.BlockSpec((None, 1, KV_LORA), lambda s, pt: (s, 0, 0)),
                  pl.BlockSpec((None, 1, LANES), lambda s, pt: (s, 0, 0)),
                  pl.BlockSpec(memory_space=pl.ANY), pl.BlockSpec(memory_space=pl.ANY)],
        out_specs=pl.BlockSpec((1, HEAD_PAD, KV_LORA), lambda s, pt: (s, 0, 0)),
        scratch_shapes=[pltpu.VMEM((n_slots, n_pg * page, KV_LORA), F32),
                        pltpu.VMEM((n_slots, n_pg, QK_ROPE, page), F32),
                        pltpu.SemaphoreType.DMA((2, n_slots)),
                        pltpu.VMEM((HEAD_PAD, 1), F32), pltpu.VMEM((HEAD_PAD, 1), F32),
                        pltpu.VMEM((HEAD_PAD, KV_LORA), F32)],
    )
    return pl.pallas_call(
        functools.partial(_decode_kernel, n_pg=n_pg, n_groups=n_pages // n_pg, page=page),
        grid_spec=grid_spec,
        out_shape=jax.ShapeDtypeStruct((n_s, HEAD_PAD, KV_LORA), F32),
        compiler_params=_cparams("arbitrary"),
        name="decode_attention",
    )(page_table, ql, qp, ckv_new, kpe_new, ckv_pool, kpe_pool_t)


def _uv_kernel(o_ref, wuv_ref, y_ref):
    for h in range(MLA_HEADS):
        y_ref[:, h * V_HEAD_DIM:(h + 1) * V_HEAD_DIM] = _dot(o_ref[h], wuv_ref[h]).astype(BF16)


def _uv_step(o, wuv):
    n_s = o.shape[1]
    return pl.pallas_call(
        _uv_kernel,
        grid=(1,),
        in_specs=[_resident(o.shape), _resident(wuv.shape)],
        out_specs=_resident((n_s, MLA_HEADS * V_HEAD_DIM)),
        out_shape=jax.ShapeDtypeStruct((n_s, MLA_HEADS * V_HEAD_DIM), BF16),
        compiler_params=_cparams("arbitrary"),
        name="uv_step",
    )(o, wuv)


def _wo_ln_kernel(ys_ref, ym_ref, x_ref, wo_ref, g_ref, b_ref, h_ref):
    rows = x_ref.shape[0] // WO_PARTS
    parts = [slice(r * rows, (r + 1) * rows) for r in range(WO_PARTS)]
    mixes = [_dot(ys_ref[rs, :], wo_ref[0:SSM_WIDTH, :]) + _dot(ym_ref[rs, :], wo_ref[SSM_WIDTH:, :])
             for rs in parts]
    for rs, mix in zip(parts, mixes):
        h_ref[rs, :] = _layer_norm(ALPHA * x_ref[rs, :] + mix, g_ref[...], b_ref[...])


def _wo_ln(ys, ym, x, wo, g, b, tm):
    n = x.shape[0]
    row = lambda w: pl.BlockSpec((tm, w), lambda i: (i, 0))
    return pl.pallas_call(
        _wo_ln_kernel,
        grid=(n // tm,),
        in_specs=[row(SSM_WIDTH), row(SSM_WIDTH), row(D_MODEL), _resident(wo.shape), _resident(g.shape),
                  _resident(b.shape)],
        out_specs=row(D_MODEL),
        out_shape=jax.ShapeDtypeStruct((n, D_MODEL), F32),
        compiler_params=_cparams("arbitrary"),
        name="wo_ln",
    )(ys, ym, x, wo, g, b)


def _ffn_begin(j, h_ref, hb_s, acc_s):
    @pl.when(j == 0)
    def _():
        hb_s[...] = h_ref[...].astype(BF16)
        acc_s[...] = jnp.zeros_like(acc_s)


def _ffn_gate_up(hb_s, wg_ref, wu_ref):
    hb = hb_s[...]
    w = wg_ref.shape[1] // FFN_PARTS
    parts = [slice(c * w, (c + 1) * w) for c in range(FFN_PARTS)]
    return [(cs, _dot(hb, wg_ref[:, cs]), _dot(hb, wu_ref[:, cs])) for cs in parts]


def _ffn_end(j, h_ref, g2_ref, b2_ref, y_ref, acc_s):
    @pl.when(j == pl.num_programs(1) - 1)
    def _():
        y_ref[...] = _layer_norm(ALPHA * h_ref[...] + acc_s[...], g2_ref[...], b2_ref[...])


def _ffn_prompt_kernel(h_ref, wg_ref, wu_ref, wd_ref, cw_ref, cb_ref, g2_ref, b2_ref, y_ref, gl_ref,
                       hb_s, acc_s, ext_s, carry_s, *, blocks_per_seq):
    i = pl.program_id(0)
    j = pl.program_id(1)
    tm = h_ref.shape[0]
    _ffn_begin(j, h_ref, hb_s, acc_s)

    @pl.when(i % blocks_per_seq == 0)
    def _():
        carry_s[j] = jnp.zeros(carry_s.shape[1:], F32)

    for cs, g, u in _ffn_gate_up(hb_s, wg_ref, wu_ref):
        ext_s[0:SUBLANES, cs] = carry_s[j, :, cs]
        ext_s[SUBLANES:, cs] = g
        gc = cb_ref[:, cs] + cw_ref[FFN_CONV - 1:FFN_CONV, cs] * g
        for k in range(1, FFN_CONV):
            gc = gc + cw_ref[FFN_CONV - 1 - k:FFN_CONV - k, cs] * ext_s[pl.ds(SUBLANES - k, tm), cs]
        tail = ext_s[tm:tm + SUBLANES, cs]
        carry_s[j, :, cs] = tail
        gl_ref[0, :, cs] = tail
        acc_s[...] += _dot((_silu(gc) * u).astype(BF16), wd_ref[cs, :])
    _ffn_end(j, h_ref, g2_ref, b2_ref, y_ref, acc_s)


def _ffn_step_kernel(h_ref, wg_ref, wu_ref, wd_ref, cw_ref, cb_ref, p2_ref, p1_ref, g2_ref, b2_ref,
                     y_ref, gout_ref, hb_s, acc_s):
    j = pl.program_id(1)
    _ffn_begin(j, h_ref, hb_s, acc_s)
    for cs, g, u in _ffn_gate_up(hb_s, wg_ref, wu_ref):
        gout_ref[:, cs] = g
        gc = (cb_ref[:, cs] + cw_ref[0:1, cs] * p2_ref[:, cs] + cw_ref[1:2, cs] * p1_ref[:, cs]
              + cw_ref[2:3, cs] * g)
        acc_s[...] += _dot((_silu(gc) * u).astype(BF16), wd_ref[cs, :])
    _ffn_end(j, h_ref, g2_ref, b2_ref, y_ref, acc_s)


def _ffn_specs(tm, tf):
    return [pl.BlockSpec((tm, D_MODEL), lambda i, j: (i, 0)),
            pl.BlockSpec((D_MODEL, tf), lambda i, j: (0, j)),
            pl.BlockSpec((D_MODEL, tf), lambda i, j: (0, j + D_FF // tf)),
            pl.BlockSpec((tf, D_MODEL), lambda i, j: (j, 0)),
            pl.BlockSpec((FFN_CONV, tf), lambda i, j: (0, j)),
            pl.BlockSpec((1, tf), lambda i, j: (0, j))]


def _ffn_prompt(h, w_in, w_down, cw, cb, g2, b2, b, t, tm, tf):
    n = h.shape[0]
    nj = D_FF // tf
    bps = t // tm
    vec = pl.BlockSpec((1, D_MODEL), lambda i, j: (0, 0))
    return pl.pallas_call(
        functools.partial(_ffn_prompt_kernel, blocks_per_seq=bps),
        grid=(n // tm, nj),
        in_specs=_ffn_specs(tm, tf) + [vec, vec],
        out_specs=[pl.BlockSpec((tm, D_MODEL), lambda i, j: (i, 0)),
                   pl.BlockSpec((1, SUBLANES, tf), lambda i, j: (i, 0, j))],
        out_shape=[jax.ShapeDtypeStruct((n, D_MODEL), F32), jax.ShapeDtypeStruct((n // tm, SUBLANES, D_FF), F32)],
        scratch_shapes=[pltpu.VMEM((tm, D_MODEL), BF16), pltpu.VMEM((tm, D_MODEL), F32),
                        pltpu.VMEM((tm + SUBLANES, tf), F32), pltpu.VMEM((nj, SUBLANES, tf), F32)],
        compiler_params=_cparams("arbitrary", "arbitrary"),
        name="ffn_prompt",
    )(h, w_in, w_in, w_down, cw, cb, g2, b2)


def _ffn_step(h, w_in, w_down, cw, cb, p2, p1, g2, b2, tf):
    n = h.shape[0]
    nj = D_FF // tf
    vec = pl.BlockSpec((1, D_MODEL), lambda i, j: (0, 0))
    col = pl.BlockSpec((n, tf), lambda i, j: (0, j))
    return pl.pallas_call(
        _ffn_step_kernel,
        grid=(1, nj),
        in_specs=_ffn_specs(n, tf) + [col, col, vec, vec],
        out_specs=[pl.BlockSpec((n, D_MODEL), lambda i, j: (0, 0)), col],
        out_shape=[jax.ShapeDtypeStruct((n, D_MODEL), F32), jax.ShapeDtypeStruct((n, D_FF), F32)],
        scratch_shapes=[pltpu.VMEM((n, D_MODEL), BF16), pltpu.VMEM((n, D_MODEL), F32)],
        compiler_params=_cparams("arbitrary", "arbitrary"),
        name="ffn_step",
    )(h, w_in, w_in, w_down, cw, cb, p2, p1, g2, b2)


def _rope_table(pos):
    inv_freq = ROPE_THETA ** (-jnp.arange(0, QK_ROPE, 2, dtype=F32) / QK_ROPE)
    ang = pos.astype(F32)[:, None] * inv_freq[None, :]
    c, s = jnp.cos(ang), jnp.sin(ang)
    return jnp.concatenate([c, c, -s, s], axis=1)


def _swap_halves(w):
    half = w.shape[-1] // 2
    return jnp.concatenate([w[..., half:], w[..., :half]], axis=-1)


def _pad_lanes(v):
    return jnp.pad(v.reshape(1, -1).astype(F32), ((0, 0), (0, LANES - v.shape[-1])))


def _tile(n, cap):
    t = min(n, cap)
    assert n % t == 0
    return t


def kernel(x_prompt, x_sample, cache_ckv, cache_kpe, page_table, state_ssm, state_conv, state_ffn_conv,
           w_in, conv_w, conv_b, dt_bias, a_log, d_skip, ssm_norm_w, q_norm_w, kv_norm_w,
           w_uq, w_uk, w_uv, w_o, ln1_g, ln1_b, w_ffn_in, ffn_conv_w, ffn_conv_b, w_ffn_down, ln2_g, ln2_b):
    assert w_in.shape[0] == DEPTH == 1 and x_sample.shape[1] == 1
    b, t, _ = x_prompt.shape
    n_s = x_sample.shape[0]
    n_pages = page_table.shape[1]
    past_len = n_pages * cache_ckv.shape[2]
    assert t % SSM_CHUNK == 0

    wi = w_in[0]
    w_kpe = wi[:, OFF_CKV:]
    w_misc = jnp.concatenate([w_kpe, _swap_halves(w_kpe), wi[:, OFF_XBC:OFF_DT],
                              jnp.zeros((D_MODEL, LANES - SSM_HEADS), F32)], axis=1)
    w1 = (wi[:, :OFF_XBC].astype(BF16), wi[:, OFF_DT:OFF_CKV].astype(BF16), w_misc.astype(BF16))
    uq = w_uq[0]
    uq_pe = uq[:, :, QK_NOPE:]
    wq = jnp.concatenate([uq[:, :, :QK_NOPE], uq_pe, _swap_halves(uq_pe)], axis=-1)
    wq = wq.reshape(Q_LORA, MLA_HEADS * QK_PAD).astype(BF16)
    wuk = w_uk[0].reshape(KV_LORA, MLA_HEADS * QK_NOPE).astype(BF16)
    wuv = w_uv[0].reshape(KV_LORA, MLA_HEADS * V_HEAD_DIM).astype(BF16)
    wukt = jnp.transpose(w_uk[0], (1, 2, 0)).astype(BF16)
    wuv_h = jnp.transpose(w_uv[0], (1, 0, 2)).astype(BF16)
    wo = w_o[0].astype(BF16)
    wf_in = w_ffn_in[0].astype(BF16)
    wf_down = w_ffn_down[0].astype(BF16)
    row = lambda v: v.reshape(1, -1).astype(F32)
    dtb, alog = _pad_lanes(dt_bias[0]), _pad_lanes(a_log[0])
    dsk = row(jnp.repeat(d_skip[0], SSM_HEAD_DIM))
    nw, qg, kg = row(ssm_norm_w[0]), row(q_norm_w[0]), row(kv_norm_w[0])
    cw, cb = conv_w[0], row(conv_b[0])
    fcw, fcb = ffn_conv_w[0], row(ffn_conv_b[0])
    g1, b1, g2, b2 = row(ln1_g[0]), row(ln1_b[0]), row(ln2_g[0]), row(ln2_b[0])
    expand = (jnp.arange(SSM_WIDTH)[None, :] // SSM_HEAD_DIM == jnp.arange(LANES)[:, None]).astype(BF16)
    tril = (jnp.arange(SSM_CHUNK)[:, None] >= jnp.arange(SSM_CHUNK)[None, :]).astype(BF16)
    tab_p = _rope_table(jnp.arange(t))
    tab_s = _rope_table(jnp.full((n_s,), past_len))

    tm = _tile(t, ROW_TILE)
    xp = x_prompt.reshape(b * t, D_MODEL)
    z, xbc, cq, ckv, kpe, kpad, dt = _in_proj(xp, w1, tab_p, dtb, qg, kg, tm)
    y_ssd, h_fin = _ssd_prompt(xbc, z, dt, cw, cb, alog, dsk, nw, expand, tril, b, t)
    tq = _tile(tm, FLASH_TILE)
    q, k, vt = _qkv_prompt(cq, ckv, kpad, tab_p, wq, wuk, wuv, b, t, tm, tq)
    y_mla = _flash_prompt(q, k, vt, tq).reshape(b * t, MLA_HEADS * V_HEAD_DIM)
    h1 = _wo_ln(y_ssd, y_mla, xp, wo, g1, b1, tm)
    tf = FF_TILE
    y_p, g_last = _ffn_prompt(h1, wf_in, wf_down, fcw, fcb, g2, b2, b, t, tm, tf)

    xs_ = x_sample.reshape(n_s, D_MODEL)
    z_s, xbc_s, cq_s, ckv_s, kpe_s, kpad_s, dt_s = _in_proj(xs_, w1, tab_s, dtb, qg, kg, n_s)
    sc = jnp.transpose(state_conv[0], (1, 0, 2))
    y_ssd_s, st_new = _ssd_step(xbc_s, sc, z_s, dt_s, cw, cb, alog, dsk, nw, expand,
                                state_ssm[0].reshape(n_s, SSM_WIDTH, SSM_STATE))
    ql, qp = _q_step(cq_s, tab_s, wq, wukt)
    pad_heads = lambda a: jnp.pad(jnp.transpose(a, (1, 0, 2)), ((0, 0), (0, HEAD_PAD - MLA_HEADS), (0, 0)))
    o_lat = _decode_attention(page_table, pad_heads(ql), pad_heads(qp), ckv_s.reshape(n_s, 1, KV_LORA),
                              kpad_s.astype(F32).reshape(n_s, 1, LANES), cache_ckv[0],
                              jnp.swapaxes(cache_kpe[0], 1, 2))
    y_mla_s = _uv_step(jnp.transpose(o_lat[:, :MLA_HEADS], (1, 0, 2)).astype(BF16), wuv_h)
    h1_s = _wo_ln(y_ssd_s, y_mla_s, xs_, wo, g1, b1, n_s)
    fbuf = state_ffn_conv[0]
    y_s, g_s = _ffn_step(h1_s, wf_in, wf_down, fcw, fcb, fbuf[:, 0], fbuf[:, 1], g2, b2, tf)

    lead = lambda a: a[None]
    return (y_p.reshape(b, t, D_MODEL),
            y_s.reshape(n_s, 1, D_MODEL),
            lead(ckv.reshape(b, t, KV_LORA)),
            lead(kpe.reshape(b, t, QK_ROPE)),
            lead(h_fin.reshape(b, SSM_HEADS, SSM_HEAD_DIM, SSM_STATE)),
            lead(xbc.reshape(b, t, CONV_DIM)[:, t - (SSM_CONV - 1):]),
            lead(g_last.reshape(b, t // tm, SUBLANES, D_FF)[:, -1, SUBLANES - (FFN_CONV - 1):]),
            lead(ckv_s.reshape(n_s, 1, KV_LORA)),
            lead(kpe_s.reshape(n_s, 1, QK_ROPE)),
            lead(st_new.reshape(n_s, SSM_HEADS, SSM_HEAD_DIM, SSM_STATE)),
            lead(jnp.concatenate([state_conv[0][:, 1:], xbc_s[:, None]], axis=1)),
            lead(jnp.concatenate([fbuf[:, 1:], g_s[:, None]], axis=1)))
```

```python
import functools
import math

import jax
import jax.numpy as jnp
from jax import lax
from jax.experimental import pallas as pl
from jax.experimental.pallas import tpu as pltpu

F32 = jnp.float32
BF16 = jnp.bfloat16

D_MODEL = 2048
SSM_WIDTH = 1024
SSM_HEAD_DIM = 64
SSM_HEADS = 16
SSM_GROUPS = 2
SSM_STATE = 128
SSM_CONV = 4
SSM_CHUNK = 128
CONV_DIM = SSM_WIDTH + 2 * SSM_GROUPS * SSM_STATE
GROUP_WIDTH = SSM_WIDTH // SSM_GROUPS
MLA_HEADS = 8
V_HEAD_DIM = 128
QK_NOPE = 128
QK_ROPE = 64
Q_LORA = 512
KV_LORA = 512
ROPE_THETA = 10000.0
ATTN_SCALE = (QK_NOPE + QK_ROPE) ** -0.5
D_FF = 5632
FFN_CONV = 3
LN_EPS = 1e-5
RMS_EPS = 1e-6
DEPTH = 1
ALPHA = (2.0 * DEPTH) ** 0.25
OFF_Z = SSM_WIDTH
OFF_XBC = OFF_Z + CONV_DIM
OFF_DT = OFF_XBC + SSM_HEADS
OFF_CQ = OFF_DT + Q_LORA
OFF_CKV = OFF_CQ + KV_LORA

LANES = 128
SUBLANES = 8
QK_PAD = 2 * LANES
HEAD_PAD = 16
VT_ROWS = V_HEAD_DIM + 16
NEG_BIG = -1e30
VMEM_LIMIT = 56 * 1024 * 1024
ROW_TILE = 512
FF_TILE = 512
FFN_ROW_TILE = 1024
FFN_PARTS = 2
WO_PARTS = 2
SSD_STEP_SAMPLES = 4
FLASH_HEADS = 4
FLASH_TILE = 512
DECODE_PAGES = 16
DECODE_SLOTS = 3
EXP2_SCALE = ATTN_SCALE * math.log2(math.e)


def _cparams(*sem):
    return pltpu.CompilerParams(dimension_semantics=sem, vmem_limit_bytes=VMEM_LIMIT)


def _dot(a, b):
    return jnp.dot(a, b, preferred_element_type=F32)


def _dot_nt(a, b):
    return lax.dot_general(a, b, (((1,), (1,)), ((), ())), preferred_element_type=F32)


def _split3(x):
    hi = x.astype(BF16)
    r = x - hi.astype(F32)
    mid = r.astype(BF16)
    lo = (r - mid.astype(F32)).astype(BF16)
    return hi, mid, lo


def _dot_exact_lhs(x, sel):
    hi, mid, lo = _split3(x)
    return _dot(hi, sel) + _dot(mid, sel) + _dot(lo, sel)


def _dot_exact_rhs(sel, x):
    hi, mid, lo = _split3(x)
    return _dot(sel, hi) + _dot(sel, mid) + _dot(sel, lo)


def _silu(x):
    hx = 0.5 * x
    return hx + hx * jnp.tanh(hx)


def _softplus(x):
    return jnp.maximum(x, 0.0) + jnp.log1p(jnp.exp(-jnp.abs(x)))


def _rms(x, g):
    r = lax.rsqrt(jnp.mean(x * x, axis=-1, keepdims=True) + RMS_EPS)
    return x * r * g


def _layer_norm(v, g, b):
    mu = jnp.mean(v, axis=-1, keepdims=True)
    d = v - mu
    var = jnp.mean(d * d, axis=-1, keepdims=True)
    return d * lax.rsqrt(var + LN_EPS) * g + b


def _rope_fold(t):
    return t + pltpu.roll(t, QK_ROPE, 1)


def _resident(shape):
    nd = len(shape)
    return pl.BlockSpec(shape, lambda *_: (0,) * nd)


def _in_proj_kernel(x_ref, wssd_ref, wlat_ref, wmisc_ref, tab_ref, dtb_ref, qg_ref, kg_ref,
                    z_ref, xbc_ref, cq_ref, ckv_ref, kpe_ref, kpad_ref, dt_ref):
    xb = x_ref[...].astype(BF16)
    z_ref[...] = _dot(xb, wssd_ref[:, :SSM_WIDTH])
    xbc_ref[...] = _dot(xb, wssd_ref[:, SSM_WIDTH:])
    cq_ref[...] = _rms(_dot(xb, wlat_ref[:, :Q_LORA]), qg_ref[...]).astype(BF16)
    ckv_ref[...] = _rms(_dot(xb, wlat_ref[:, Q_LORA:]), kg_ref[...])
    kr = _rope_fold(_dot(xb, wmisc_ref[:, :LANES]) * tab_ref[...])
    kpe_ref[...] = kr[:, :QK_ROPE]
    lane = lax.broadcasted_iota(jnp.int32, kr.shape, 1)
    kpad_ref[...] = jnp.where(lane < QK_ROPE, kr, 0.0).astype(BF16)
    dt_ref[...] = _softplus(_dot(xb, wmisc_ref[:, LANES:]) + dtb_ref[...])


def _in_proj(x, w1, tab, dtb, qg, kg, tm):
    n = x.shape[0]
    n_tab = tab.shape[0] // tm
    row = lambda w: pl.BlockSpec((tm, w), lambda i: (i, 0))
    return pl.pallas_call(
        _in_proj_kernel,
        grid=(n // tm,),
        in_specs=[row(D_MODEL)] + [_resident(w.shape) for w in w1] + [
                  pl.BlockSpec((tm, LANES), lambda i: (i % n_tab, 0)),
                  _resident(dtb.shape), _resident(qg.shape), _resident(kg.shape)],
        out_specs=[row(SSM_WIDTH), row(CONV_DIM), row(Q_LORA), row(KV_LORA), row(QK_ROPE), row(LANES),
                   row(LANES)],
        out_shape=[jax.ShapeDtypeStruct((n, SSM_WIDTH), F32), jax.ShapeDtypeStruct((n, CONV_DIM), F32),
                   jax.ShapeDtypeStruct((n, Q_LORA), BF16), jax.ShapeDtypeStruct((n, KV_LORA), F32),
                   jax.ShapeDtypeStruct((n, QK_ROPE), F32), jax.ShapeDtypeStruct((n, LANES), BF16),
                   jax.ShapeDtypeStruct((n, LANES), F32)],
        compiler_params=_cparams("arbitrary"),
        name="in_proj",
    )(x, *w1, tab, dtb, qg, kg)


def _gate_and_norm(y, z, nw):
    yg = y * _silu(z)
    parts = []
    for g in range(SSM_GROUPS):
        v = yg[:, g * GROUP_WIDTH:(g + 1) * GROUP_WIDTH]
        parts.append(v * lax.rsqrt(jnp.mean(v * v, axis=-1, keepdims=True) + RMS_EPS))
    return jnp.concatenate(parts, axis=1) * nw


def _ssd_kernel(xbc_ref, z_ref, dt_ref, cw_ref, cb_ref, alog_ref, dsk_ref, nw_ref, e_ref, tril_ref,
                y_ref, hout_ref, ext_ref, ht_ref):
    c = pl.program_id(1)
    L = SSM_CHUNK

    @pl.when(c == 0)
    def _():
        ext_ref[0:SUBLANES, :] = jnp.zeros((SUBLANES, CONV_DIM), F32)
        ht_ref[...] = jnp.zeros_like(ht_ref)

    ext_ref[SUBLANES:SUBLANES + L, :] = xbc_ref[...]
    conv = cb_ref[...] + cw_ref[SSM_CONV - 1:SSM_CONV, :] * xbc_ref[...]
    for k in range(1, SSM_CONV):
        conv = conv + cw_ref[SSM_CONV - 1 - k:SSM_CONV - k, :] * ext_ref[pl.ds(SUBLANES - k, L), :]
    ext_ref[0:SUBLANES, :] = ext_ref[L:L + SUBLANES, :]
    xc = _silu(conv)
    xs = xc[:, :SSM_WIDTH]
    bm = xc[:, SSM_WIDTH:SSM_WIDTH + SSM_GROUPS * SSM_STATE]
    cm = xc[:, SSM_WIDTH + SSM_GROUPS * SSM_STATE:]

    dt = dt_ref[...]
    da = dt * (-jnp.exp(alog_ref[...]))
    acs = _dot_exact_rhs(tril_ref[...], da)
    e = e_ref[...]
    acs_x = _dot_exact_lhs(acs, e)
    dt_x = _dot_exact_lhs(dt, e)
    last_x = acs_x[L - 1:L, :]
    xw = xs * dt_x * jnp.exp(last_x - acs_x)
    exp_acs = jnp.exp(acs_x)
    acs_t = acs.T
    dt_t = dt.T

    row = lax.broadcasted_iota(jnp.int32, (L, L), 0)
    col = lax.broadcasted_iota(jnp.int32, (L, L), 1)
    causal = row >= col
    lower_half = col < SSM_HEAD_DIM

    hprev = ht_ref[...].astype(BF16)
    y_parts = []
    heads_per_group = SSM_HEADS // SSM_GROUPS
    for g in range(SSM_GROUPS):
        gs = slice(g * GROUP_WIDTH, (g + 1) * GROUP_WIDTH)
        bg = bm[:, g * SSM_STATE:(g + 1) * SSM_STATE]
        cg = cm[:, g * SSM_STATE:(g + 1) * SSM_STATE].astype(BF16)
        cb = _dot_nt(cg, bg.astype(BF16))
        y_off = _dot(cg, hprev[:, gs]) * exp_acs[:, gs]
        for jj in range(heads_per_group // 2):
            j = g * (heads_per_group // 2) + jj
            xp = xs[:, j * LANES:(j + 1) * LANES]
            yp = None
            for h, xh in ((2 * j, jnp.where(lower_half, xp, 0.0)), (2 * j + 1, jnp.where(lower_half, 0.0, xp))):
                diff = acs[:, h:h + 1] - acs_t[h:h + 1, :]
                dec = jnp.exp(jnp.where(causal, diff, NEG_BIG))
                m = (cb * dec * dt_t[h:h + 1, :]).astype(BF16)
                t = _dot(m, xh.astype(BF16))
                yp = t if yp is None else yp + t
            y_parts.append(yp + y_off[:, jj * LANES:(jj + 1) * LANES])
        ht_ref[:, gs] = ht_ref[:, gs] * jnp.exp(last_x[:, gs]) + _dot(bg.T.astype(BF16), xw[:, gs].astype(BF16))

    y = jnp.concatenate(y_parts, axis=1) + dsk_ref[...] * xs
    y_ref[...] = _gate_and_norm(y, z_ref[...], nw_ref[...]).astype(BF16)

    @pl.when(c == pl.num_programs(1) - 1)
    def _():
        hout_ref[0] = ht_ref[...].T


def _ssd_prompt(xbc, z, dt, cw, cb, alog, dsk, nw, e, tril, b, t):
    nc = t // SSM_CHUNK
    L = SSM_CHUNK
    row = lambda w: pl.BlockSpec((L, w), lambda bi, c: (bi * nc + c, 0))
    return pl.pallas_call(
        _ssd_kernel,
        grid=(b, nc),
        in_specs=[row(CONV_DIM), row(SSM_WIDTH), row(LANES)] +
                 [_resident(a.shape) for a in (cw, cb, alog, dsk, nw, e, tril)],
        out_specs=[row(SSM_WIDTH), pl.BlockSpec((1, SSM_WIDTH, SSM_STATE), lambda bi, c: (bi, 0, 0))],
        out_shape=[jax.ShapeDtypeStruct((b * t, SSM_WIDTH), BF16),
                   jax.ShapeDtypeStruct((b, SSM_WIDTH, SSM_STATE), F32)],
        scratch_shapes=[pltpu.VMEM((L + SUBLANES, CONV_DIM), F32), pltpu.VMEM((SSM_STATE, SSM_WIDTH), F32)],
        compiler_params=_cparams("arbitrary", "arbitrary"),
        name="ssd_prompt",
    )(xbc, z, dt, cw, cb, alog, dsk, nw, e, tril)


def _ssd_step_kernel(xbc_ref, sc_ref, z_ref, dt_ref, cw_ref, cb_ref, alog_ref, dsk_ref, nw_ref, e_ref, st_ref,
                     y_ref, so_ref, xs_s, b_s, ct_s, xdt_s, da_s, yt_s):
    blk = pl.program_id(0)
    per_step = st_ref.shape[0]

    @pl.when(blk == 0)
    def _():
        conv = cb_ref[...] + cw_ref[SSM_CONV - 1:SSM_CONV, :] * xbc_ref[...]
        for k in range(SSM_CONV - 1):
            conv = conv + cw_ref[k:k + 1, :] * sc_ref[k]
        xc = _silu(conv)
        xs = xc[:, :SSM_WIDTH]
        xs_s[...] = xs
        b_s[...] = xc[:, SSM_WIDTH:SSM_WIDTH + SSM_GROUPS * SSM_STATE]
        ct_s[...] = xc[:, SSM_WIDTH + SSM_GROUPS * SSM_STATE:].T
        dt = dt_ref[...]
        xdt_s[...] = (xs * _dot_exact_lhs(dt, e_ref[...])).T.astype(BF16)
        da_s[...] = jnp.exp(dt * (-jnp.exp(alog_ref[...])))
        yt_s[...] = jnp.zeros_like(yt_s)

    n_samp = xs_s.shape[0]
    row = lax.broadcasted_iota(jnp.int32, (n_samp, SSM_STATE), 0)
    lane = lax.broadcasted_iota(jnp.int32, (SSM_STATE, n_samp), 1)
    heads_per_group = SSM_HEADS // SSM_GROUPS
    for k in range(per_step):
        s = blk * per_step + k
        pick_row = row == s
        decay = jnp.broadcast_to(da_s[pl.ds(s, 1), :], (SSM_STATE, SSM_STATE)).T
        b_row = b_s[pl.ds(s, 1), :]
        for g in range(SSM_GROUPS):
            gs = slice(g * GROUP_WIDTH, (g + 1) * GROUP_WIDTH)
            ns = slice(g * SSM_STATE, (g + 1) * SSM_STATE)
            eb = jnp.where(pick_row, b_row[:, ns], 0.0).astype(BF16)
            upd = _dot(xdt_s[gs, :], eb)
            for hh in range(heads_per_group):
                h = g * heads_per_group + hh
                rs = slice(h * SSM_HEAD_DIM, (h + 1) * SSM_HEAD_DIM)
                so_ref[k, rs, :] = (st_ref[k, rs, :] * decay[h:h + 1, :]
                                    + upd[hh * SSM_HEAD_DIM:(hh + 1) * SSM_HEAD_DIM, :])
            ce = jnp.where(lane == s, ct_s[ns, :], 0.0).astype(BF16)
            yt_s[gs, :] += _dot(so_ref[k, gs, :].astype(BF16), ce)

    @pl.when(blk == pl.num_programs(0) - 1)
    def _():
        y = yt_s[...].T + dsk_ref[...] * xs_s[...]
        y_ref[...] = _gate_and_norm(y, z_ref[...], nw_ref[...]).astype(BF16)


def _ssd_step(xbc, sc, z, dt, cw, cb, alog, dsk, nw, e, state):
    n_s = xbc.shape[0]
    per_step = _tile(n_s, SSD_STEP_SAMPLES)
    st_spec = pl.BlockSpec((per_step, SSM_WIDTH, SSM_STATE), lambda s: (s, 0, 0))
    return pl.pallas_call(
        _ssd_step_kernel,
        grid=(n_s // per_step,),
        in_specs=[_resident(a.shape) for a in (xbc, sc, z, dt, cw, cb, alog, dsk, nw, e)] + [st_spec],
        out_specs=[_resident((n_s, SSM_WIDTH)), st_spec],
        out_shape=[jax.ShapeDtypeStruct((n_s, SSM_WIDTH), BF16),
                   jax.ShapeDtypeStruct((n_s, SSM_WIDTH, SSM_STATE), F32)],
        scratch_shapes=[pltpu.VMEM((n_s, SSM_WIDTH), F32), pltpu.VMEM((n_s, SSM_GROUPS * SSM_STATE), F32),
                        pltpu.VMEM((SSM_GROUPS * SSM_STATE, n_s), F32), pltpu.VMEM((SSM_WIDTH, n_s), BF16),
                        pltpu.VMEM((n_s, LANES), F32), pltpu.VMEM((SSM_WIDTH, n_s), F32)],
        compiler_params=_cparams("arbitrary"),
        name="ssd_step",
    )(xbc, sc, z, dt, cw, cb, alog, dsk, nw, e, state)


def _rope_q_head(qh, tab, lane):
    r = _rope_fold(qh[:, QK_NOPE:] * tab)
    return qh[:, :QK_NOPE], jnp.where(lane < QK_ROPE, r, 0.0)


def _qkv_kernel(cq_ref, ckv_ref, kpad_ref, tab_ref, wq_ref, wuk_ref, wuv_ref, q_ref, k_ref, v_ref):
    q = _dot(cq_ref[...], wq_ref[...])
    ckv = ckv_ref[...].astype(BF16)
    kn = _dot(ckv, wuk_ref[...])
    v = _dot(ckv, wuv_ref[...])
    tab = tab_ref[...]
    kpad = kpad_ref[...]
    lane = lax.broadcasted_iota(jnp.int32, tab.shape, 1)
    for h in range(MLA_HEADS):
        nope, pe = _rope_q_head(q[:, h * QK_PAD:(h + 1) * QK_PAD], tab, lane)
        q_ref[0, h] = (jnp.concatenate([nope, pe], axis=1) * EXP2_SCALE).astype(BF16)
        k_ref[0, h] = jnp.concatenate([kn[:, h * QK_NOPE:(h + 1) * QK_NOPE].astype(BF16), kpad], axis=1)
        vt = v[:, h * V_HEAD_DIM:(h + 1) * V_HEAD_DIM].T.astype(BF16)
        tkv = v_ref.shape[-1]
        extra = lax.broadcasted_iota(jnp.int32, (VT_ROWS - V_HEAD_DIM, tkv), 0)
        ones_row = jnp.where(extra == 0, 1.0, 0.0).astype(BF16)
        for c in range(v_ref.shape[2]):
            v_ref[0, h, c, 0:V_HEAD_DIM, :] = vt[:, c * tkv:(c + 1) * tkv]
            v_ref[0, h, c, V_HEAD_DIM:, :] = ones_row


def _qkv_prompt(cq, ckv, kpad, tab, wq, wuk, wuv, b, t, tm, tkv):
    nt = t // tm
    nc = tm // tkv
    row = lambda w: pl.BlockSpec((tm, w), lambda bi, i: (bi * nt + i, 0))
    head = lambda w: pl.BlockSpec((1, MLA_HEADS, tm, w), lambda bi, i: (bi, 0, i, 0))
    return pl.pallas_call(
        _qkv_kernel,
        grid=(b, nt),
        in_specs=[row(Q_LORA), row(KV_LORA), row(LANES), pl.BlockSpec((tm, LANES), lambda bi, i: (i, 0)),
                  _resident(wq.shape), _resident(wuk.shape), _resident(wuv.shape)],
        out_specs=[head(QK_PAD), head(QK_PAD),
                   pl.BlockSpec((1, MLA_HEADS, nc, VT_ROWS, tkv), lambda bi, i: (bi, 0, i, 0, 0))],
        out_shape=[jax.ShapeDtypeStruct((b, MLA_HEADS, t, QK_PAD), BF16),
                   jax.ShapeDtypeStruct((b, MLA_HEADS, t, QK_PAD), BF16),
                   jax.ShapeDtypeStruct((b, MLA_HEADS, t // tkv, VT_ROWS, tkv), BF16)],
        compiler_params=_cparams("arbitrary", "arbitrary"),
        name="qkv_prompt",
    )(cq, ckv, kpad, tab, wq, wuk, wuv)


def _q_step_kernel(cq_ref, tab_ref, wq_ref, wukt_ref, ql_ref, qp_ref):
    q = _dot(cq_ref[...], wq_ref[...])
    tab = tab_ref[...]
    lane = lax.broadcasted_iota(jnp.int32, tab.shape, 1)
    for h in range(MLA_HEADS):
        nope, pe = _rope_q_head(q[:, h * QK_PAD:(h + 1) * QK_PAD], tab, lane)
        ql_ref[h] = _dot(nope.astype(BF16), wukt_ref[h]).astype(BF16)
        qp_ref[h] = pe.astype(BF16)


def _q_step(cq, tab, wq, wukt):
    n_s = cq.shape[0]
    return pl.pallas_call(
        _q_step_kernel,
        grid=(1,),
        in_specs=[_resident(a.shape) for a in (cq, tab, wq, wukt)],
        out_specs=[_resident((MLA_HEADS, n_s, KV_LORA)), _resident((MLA_HEADS, n_s, LANES))],
        out_shape=[jax.ShapeDtypeStruct((MLA_HEADS, n_s, KV_LORA), BF16),
                   jax.ShapeDtypeStruct((MLA_HEADS, n_s, LANES), BF16)],
        compiler_params=_cparams("arbitrary"),
        name="q_step",
    )(cq, tab, wq, wukt)


def _softmax_update(s, m_ref, l_ref):
    m_prev = m_ref[...]
    m_new = jnp.maximum(m_prev, jnp.max(s, axis=-1, keepdims=True))
    corr = jnp.exp2((m_prev - m_new) * EXP2_SCALE)
    p = jnp.exp2((s - m_new) * EXP2_SCALE)
    l_ref[...] = l_ref[...] * corr + jnp.sum(p, axis=-1, keepdims=True)
    m_ref[...] = m_new
    return p, corr


def _flash_kernel(q_ref, k_ref, vt_ref, o_ref, *scratch, tq, nh):
    m_s, acc_s = scratch[:nh], scratch[nh:]
    qi = pl.program_id(2)
    for h in range(nh):
        m_s[h][...] = jnp.full_like(m_s[h], NEG_BIG)
        acc_s[h][...] = jnp.zeros_like(acc_s[h])

    def step(j, masked):
        start = pl.multiple_of(j * tq, tq)
        sts = [_dot_nt(k_ref[0, h, pl.ds(start, tq), :], q_ref[0, h]) for h in range(nh)]
        ps, corrs = [], []
        for h in range(nh):
            st = sts[h]
            if masked:
                key = lax.broadcasted_iota(jnp.int32, st.shape, 0)
                qry = lax.broadcasted_iota(jnp.int32, st.shape, 1)
                st = jnp.where(key <= qry, st, NEG_BIG)
            m_prev = m_s[h][...]
            m_new = jnp.maximum(m_prev, jnp.max(st, axis=0, keepdims=True))
            corr = jnp.exp2(m_prev - m_new)
            m_s[h][...] = m_new
            ps.append(jnp.exp2(st - m_new).astype(BF16))
            corrs.append(corr)
        for h in range(nh):
            acc_s[h][...] = acc_s[h][...] * corrs[h] + _dot(vt_ref[0, h, j], ps[h])

    def body(j, carry):
        step(j, False)
        return carry

    lax.fori_loop(0, qi, body, 0)
    step(qi, True)
    for h in range(nh):
        o = acc_s[h][0:V_HEAD_DIM, :] / acc_s[h][V_HEAD_DIM:V_HEAD_DIM + 1, :]
        o_ref[0, :, h * V_HEAD_DIM:(h + 1) * V_HEAD_DIM] = o.T.astype(BF16)


def _flash_prompt(q, k, vt, tq):
    b, h, t, _ = q.shape
    nh = FLASH_HEADS
    return pl.pallas_call(
        functools.partial(_flash_kernel, tq=tq, nh=nh),
        grid=(b, h // nh, t // tq),
        in_specs=[pl.BlockSpec((1, nh, tq, QK_PAD), lambda bi, hi, qi: (bi, hi, qi, 0)),
                  pl.BlockSpec((1, nh, t, QK_PAD), lambda bi, hi, qi: (bi, hi, 0, 0)),
                  pl.BlockSpec((1, nh, t // tq, VT_ROWS, tq), lambda bi, hi, qi: (bi, hi, 0, 0, 0))],
        out_specs=pl.BlockSpec((1, tq, nh * V_HEAD_DIM), lambda bi, hi, qi: (bi, qi, hi)),
        out_shape=jax.ShapeDtypeStruct((b, t, h * V_HEAD_DIM), BF16),
        scratch_shapes=([pltpu.VMEM((1, tq), F32)] * nh + [pltpu.VMEM((VT_ROWS, tq), F32)] * nh),
        compiler_params=_cparams("arbitrary", "arbitrary", "arbitrary"),
        name="flash_prompt",
    )(q, k, vt)


def _decode_kernel(pt_ref, ql_ref, qp_ref, cn_ref, kn_ref, ckv_hbm, kpe_hbm, o_ref,
                   kc_buf, kp_buf, sem, m_s, l_s, acc_s, *, n_pg, n_groups, page):
    s = pl.program_id(0)
    n_slots = kc_buf.shape[0]
    ahead = n_slots - 1
    total = pl.num_programs(0) * n_groups

    def copies(samp, grp, slot):
        out = []
        for i in range(n_pg):
            pg = pt_ref[samp, grp * n_pg + i]
            out.append(pltpu.make_async_copy(ckv_hbm.at[pg], kc_buf.at[slot, pl.ds(i * page, page), :],
                                             sem.at[0, slot]))
            out.append(pltpu.make_async_copy(kpe_hbm.at[pg], kp_buf.at[slot, i], sem.at[1, slot]))
        return out

    def start(samp, grp, slot):
        for c in copies(samp, grp, slot):
            c.start()

    @pl.when(s == 0)
    def _():
        for t in range(ahead):
            start(t // n_groups, t % n_groups, t % n_slots)

    ql = ql_ref[0]
    qp = qp_ref[0][:, :QK_ROPE]
    cn = cn_ref[...].astype(BF16).astype(F32)
    kn = kn_ref[...].astype(BF16).astype(F32)
    m_s[...] = (jnp.sum(ql.astype(F32) * cn, axis=-1, keepdims=True)
                + jnp.sum(qp_ref[0].astype(F32) * kn, axis=-1, keepdims=True))
    l_s[...] = jnp.ones_like(l_s)
    acc_s[...] = jnp.broadcast_to(cn, acc_s.shape)

    def group(g, carry):
        t = s * n_groups + g
        slot = lax.rem(t, n_slots)
        g_next = g + ahead
        wraps = g_next >= n_groups

        @pl.when(t + ahead < total)
        def _():
            start(jnp.where(wraps, s + 1, s), jnp.where(wraps, g_next - n_groups, g_next),
                  lax.rem(t + ahead, n_slots))

        for c in copies(s, g, slot):
            c.wait()
        kc = kc_buf[slot].astype(BF16)
        s_pe = [_dot(qp, kp_buf[slot, i].astype(BF16)) for i in range(n_pg)]
        sc = _dot_nt(ql, kc) + jnp.concatenate(s_pe, axis=1)
        p, corr = _softmax_update(sc, m_s, l_s)
        acc_s[...] = acc_s[...] * corr + _dot(p.astype(BF16), kc)
        return carry

    lax.fori_loop(0, n_groups, group, 0)
    o_ref[0] = acc_s[...] / l_s[...]


def _decode_attention(page_table, ql, qp, ckv_new, kpe_new, ckv_pool, kpe_pool_t):
    n_s, n_pages = page_table.shape
    page = ckv_pool.shape[1]
    n_pg = min(DECODE_PAGES, n_pages // 2)
    n_slots = DECODE_SLOTS
    assert n_pages % n_pg == 0 and n_slots - 1 <= n_pages // n_pg
    grid_spec = pltpu.PrefetchScalarGridSpec(
        num_scalar_prefetch=1,
        grid=(n_s,),
        in_specs=[pl.BlockSpec((1, HEAD_PAD, KV_LORA), lambda s, pt: (s, 0, 0)),
                  pl.BlockSpec((1, HEAD_PAD, LANES), lambda s, pt: (s, 0, 0)),
                  pl.BlockSpec((None, 1, KV_LORA), lambda s, pt: (s, 0, 0)),
                  pl.BlockSpec((None, 1, LANES), lambda s, pt: (s, 0, 0)),
                  pl.BlockSpec(memory_space=pl.ANY), pl.BlockSpec(memory_space=pl.ANY)],
        out_specs=pl.BlockSpec((1, HEAD_PAD, KV_LORA), lambda s, pt: (s, 0, 0)),
        scratch_shapes=[pltpu.VMEM((n_slots, n_pg * page, KV_LORA), F32),
                        pltpu.VMEM((n_slots, n_pg, QK_ROPE, page), F32),
                        pltpu.SemaphoreType.DMA((2, n_slots)),
                        pltpu.VMEM((HEAD_PAD, 1), F32), pltpu.VMEM((HEAD_PAD, 1), F32),
                        pltpu.VMEM((HEAD_PAD, KV_LORA), F32)],
    )
    return pl.pallas_call(
        functools.partial(_decode_kernel, n_pg=n_pg, n_groups=n_pages // n_pg, page=page),
        grid_spec=grid_spec,
        out_shape=jax.ShapeDtypeStruct((n_s, HEAD_PAD, KV_LORA), F32),
        compiler_params=_cparams("arbitrary"),
        name="decode_attention",
    )(page_table, ql, qp, ckv_new, kpe_new, ckv_pool, kpe_pool_t)


def _uv_kernel(o_ref, wuv_ref, y_ref):
    for h in range(MLA_HEADS):
        y_ref[:, h * V_HEAD_DIM:(h + 1) * V_HEAD_DIM] = _dot(o_ref[h], wuv_ref[h]).astype(BF16)


def _uv_step(o, wuv):
    n_s = o.shape[1]
    return pl.pallas_call(
        _uv_kernel,
        grid=(1,),
        in_specs=[_resident(o.shape), _resident(wuv.shape)],
        out_specs=_resident((n_s, MLA_HEADS * V_HEAD_DIM)),
        out_shape=jax.ShapeDtypeStruct((n_s, MLA_HEADS * V_HEAD_DIM), BF16),
        compiler_params=_cparams("arbitrary"),
        name="uv_step",
    )(o, wuv)


def _wo_ln_kernel(ys_ref, ym_ref, x_ref, wo_ref, g_ref, b_ref, h_ref):
    rows = x_ref.shape[0] // WO_PARTS
    parts = [slice(r * rows, (r + 1) * rows) for r in range(WO_PARTS)]
    mixes = [_dot(ys_ref[rs, :], wo_ref[0:SSM_WIDTH, :]) + _dot(ym_ref[rs, :], wo_ref[SSM_WIDTH:, :])
             for rs in parts]
    for rs, mix in zip(parts, mixes):
        h_ref[rs, :] = _layer_norm(ALPHA * x_ref[rs, :] + mix, g_ref[...], b_ref[...])


def _wo_ln(ys, ym, x, wo, g, b, tm):
    n = x.shape[0]
    row = lambda w: pl.BlockSpec((tm, w), lambda i: (i, 0))
    return pl.pallas_call(
        _wo_ln_kernel,
        grid=(n // tm,),
        in_specs=[row(SSM_WIDTH), row(SSM_WIDTH), row(D_MODEL), _resident(wo.shape), _resident(g.shape),
                  _resident(b.shape)],
        out_specs=row(D_MODEL),
        out_shape=jax.ShapeDtypeStruct((n, D_MODEL), F32),
        compiler_params=_cparams("arbitrary"),
        name="wo_ln",
    )(ys, ym, x, wo, g, b)


def _ffn_begin(j, h_ref, hb_s, acc_s):
    @pl.when(j == 0)
    def _():
        hb_s[...] = h_ref[...].astype(BF16)
        acc_s[...] = jnp.zeros_like(acc_s)


def _ffn_gate_up(hb_s, wg_ref, wu_ref):
    hb = hb_s[...]
    w = wg_ref.shape[1] // FFN_PARTS
    parts = [slice(c * w, (c + 1) * w) for c in range(FFN_PARTS)]
    return [(cs, _dot(hb, wg_ref[:, cs]), _dot(hb, wu_ref[:, cs])) for cs in parts]


def _ffn_end(j, h_ref, g2_ref, b2_ref, y_ref, acc_s):
    @pl.when(j == pl.num_programs(1) - 1)
    def _():
        y_ref[...] = _layer_norm(ALPHA * h_ref[...] + acc_s[...], g2_ref[...], b2_ref[...])


def _ffn_prompt_kernel(h_ref, wg_ref, wu_ref, wd_ref, cw_ref, cb_ref, g2_ref, b2_ref, y_ref, gl_ref,
                       hb_s, acc_s, ext_s, carry_s, *, blocks_per_seq):
    i = pl.program_id(0)
    j = pl.program_id(1)
    tm = h_ref.shape[0]
    _ffn_begin(j, h_ref, hb_s, acc_s)

    @pl.when(i % blocks_per_seq == 0)
    def _():
        carry_s[j] = jnp.zeros(carry_s.shape[1:], F32)

    for cs, g, u in _ffn_gate_up(hb_s, wg_ref, wu_ref):
        ext_s[0:SUBLANES, cs] = carry_s[j, :, cs]
        ext_s[SUBLANES:, cs] = g
        gc = cb_ref[:, cs] + cw_ref[FFN_CONV - 1:FFN_CONV, cs] * g
        for k in range(1, FFN_CONV):
            gc = gc + cw_ref[FFN_CONV - 1 - k:FFN_CONV - k, cs] * ext_s[pl.ds(SUBLANES - k, tm), cs]
        tail = ext_s[tm:tm + SUBLANES, cs]
        carry_s[j, :, cs] = tail
        gl_ref[0, :, cs] = tail
        acc_s[...] += _dot((_silu(gc) * u).astype(BF16), wd_ref[cs, :])
    _ffn_end(j, h_ref, g2_ref, b2_ref, y_ref, acc_s)


def _ffn_step_kernel(h_ref, wg_ref, wu_ref, wd_ref, cw_ref, cb_ref, p2_ref, p1_ref, g2_ref, b2_ref,
                     y_ref, gout_ref, hb_s, acc_s):
    j = pl.program_id(1)
    _ffn_begin(j, h_ref, hb_s, acc_s)
    for cs, g, u in _ffn_gate_up(hb_s, wg_ref, wu_ref):
        gout_ref[:, cs] = g
        gc = (cb_ref[:, cs] + cw_ref[0:1, cs] * p2_ref[:, cs] + cw_ref[1:2, cs] * p1_ref[:, cs]
              + cw_ref[2:3, cs] * g)
        acc_s[...] += _dot((_silu(gc) * u).astype(BF16), wd_ref[cs, :])
    _ffn_end(j, h_ref, g2_ref, b2_ref, y_ref, acc_s)


def _ffn_specs(tm, tf):
    return [pl.BlockSpec((tm, D_MODEL), lambda i, j: (i, 0), pipeline_mode=pl.Buffered(1)),
            pl.BlockSpec((D_MODEL, tf), lambda i, j: (0, j)),
            pl.BlockSpec((D_MODEL, tf), lambda i, j: (0, j + D_FF // tf)),
            pl.BlockSpec((tf, D_MODEL), lambda i, j: (j, 0)),
            pl.BlockSpec((FFN_CONV, tf), lambda i, j: (0, j)),
            pl.BlockSpec((1, tf), lambda i, j: (0, j))]


def _ffn_prompt(h, w_in, w_down, cw, cb, g2, b2, b, t, tm, tf):
    n = h.shape[0]
    nj = D_FF // tf
    bps = t // tm
    vec = pl.BlockSpec((1, D_MODEL), lambda i, j: (0, 0))
    return pl.pallas_call(
        functools.partial(_ffn_prompt_kernel, blocks_per_seq=bps),
        grid=(n // tm, nj),
        in_specs=_ffn_specs(tm, tf) + [vec, vec],
        out_specs=[pl.BlockSpec((tm, D_MODEL), lambda i, j: (i, 0), pipeline_mode=pl.Buffered(1)),
                   pl.BlockSpec((1, SUBLANES, tf), lambda i, j: (i, 0, j))],
        out_shape=[jax.ShapeDtypeStruct((n, D_MODEL), F32), jax.ShapeDtypeStruct((n // tm, SUBLANES, D_FF), F32)],
        scratch_shapes=[pltpu.VMEM((tm, D_MODEL), BF16), pltpu.VMEM((tm, D_MODEL), F32),
                        pltpu.VMEM((tm + SUBLANES, tf), F32), pltpu.VMEM((nj, SUBLANES, tf), F32)],
        compiler_params=_cparams("arbitrary", "arbitrary"),
        name="ffn_prompt",
    )(h, w_in, w_in, w_down, cw, cb, g2, b2)


def _ffn_step(h, w_in, w_down, cw, cb, p2, p1, g2, b2, tf):
    n = h.shape[0]
    nj = D_FF // tf
    vec = pl.BlockSpec((1, D_MODEL), lambda i, j: (0, 0))
    col = pl.BlockSpec((n, tf), lambda i, j: (0, j))
    return pl.pallas_call(
        _ffn_step_kernel,
        grid=(1, nj),
        in_specs=_ffn_specs(n, tf) + [col, col, vec, vec],
        out_specs=[pl.BlockSpec((n, D_MODEL), lambda i, j: (0, 0)), col],
        out_shape=[jax.ShapeDtypeStruct((n, D_MODEL), F32), jax.ShapeDtypeStruct((n, D_FF), F32)],
        scratch_shapes=[pltpu.VMEM((n, D_MODEL), BF16), pltpu.VMEM((n, D_MODEL), F32)],
        compiler_params=_cparams("arbitrary", "arbitrary"),
        name="ffn_step",
    )(h, w_in, w_in, w_down, cw, cb, p2, p1, g2, b2)


def _rope_table(pos):
    inv_freq = ROPE_THETA ** (-jnp.arange(0, QK_ROPE, 2, dtype=F32) / QK_ROPE)
    ang = pos.astype(F32)[:, None] * inv_freq[None, :]
    c, s = jnp.cos(ang), jnp.sin(ang)
    return jnp.concatenate([c, c, -s, s], axis=1)


def _swap_halves(w):
    half = w.shape[-1] // 2
    return jnp.concatenate([w[..., half:], w[..., :half]], axis=-1)


def _pad_lanes(v):
    return jnp.pad(v.reshape(1, -1).astype(F32), ((0, 0), (0, LANES - v.shape[-1])))


def _tile(n, cap):
    t = min(n, cap)
    assert n % t == 0
    return t


def kernel(x_prompt, x_sample, cache_ckv, cache_kpe, page_table, state_ssm, state_conv, state_ffn_conv,
           w_in, conv_w, conv_b, dt_bias, a_log, d_skip, ssm_norm_w, q_norm_w, kv_norm_w,
           w_uq, w_uk, w_uv, w_o, ln1_g, ln1_b, w_ffn_in, ffn_conv_w, ffn_conv_b, w_ffn_down, ln2_g, ln2_b):
    assert w_in.shape[0] == DEPTH == 1 and x_sample.shape[1] == 1
    b, t, _ = x_prompt.shape
    n_s = x_sample.shape[0]
    n_pages = page_table.shape[1]
    past_len = n_pages * cache_ckv.shape[2]
    assert t % SSM_CHUNK == 0

    wi = w_in[0]
    w_kpe = wi[:, OFF_CKV:]
    w_misc = jnp.concatenate([w_kpe, _swap_halves(w_kpe), wi[:, OFF_XBC:OFF_DT],
                              jnp.zeros((D_MODEL, LANES - SSM_HEADS), F32)], axis=1)
    w1 = (wi[:, :OFF_XBC].astype(BF16), wi[:, OFF_DT:OFF_CKV].astype(BF16), w_misc.astype(BF16))
    uq = w_uq[0]
    uq_pe = uq[:, :, QK_NOPE:]
    wq = jnp.concatenate([uq[:, :, :QK_NOPE], uq_pe, _swap_halves(uq_pe)], axis=-1)
    wq = wq.reshape(Q_LORA, MLA_HEADS * QK_PAD).astype(BF16)
    wuk = w_uk[0].reshape(KV_LORA, MLA_HEADS * QK_NOPE).astype(BF16)
    wuv = w_uv[0].reshape(KV_LORA, MLA_HEADS * V_HEAD_DIM).astype(BF16)
    wukt = jnp.transpose(w_uk[0], (1, 2, 0)).astype(BF16)
    wuv_h = jnp.transpose(w_uv[0], (1, 0, 2)).astype(BF16)
    wo = w_o[0].astype(BF16)
    wf_in = w_ffn_in[0].astype(BF16)
    wf_down = w_ffn_down[0].astype(BF16)
    row = lambda v: v.reshape(1, -1).astype(F32)
    dtb, alog = _pad_lanes(dt_bias[0]), _pad_lanes(a_log[0])
    dsk = row(jnp.repeat(d_skip[0], SSM_HEAD_DIM))
    nw, qg, kg = row(ssm_norm_w[0]), row(q_norm_w[0]), row(kv_norm_w[0])
    cw, cb = conv_w[0], row(conv_b[0])
    fcw, fcb = ffn_conv_w[0], row(ffn_conv_b[0])
    g1, b1, g2, b2 = row(ln1_g[0]), row(ln1_b[0]), row(ln2_g[0]), row(ln2_b[0])
    expand = (jnp.arange(SSM_WIDTH)[None, :] // SSM_HEAD_DIM == jnp.arange(LANES)[:, None]).astype(BF16)
    tril = (jnp.arange(SSM_CHUNK)[:, None] >= jnp.arange(SSM_CHUNK)[None, :]).astype(BF16)
    tab_p = _rope_table(jnp.arange(t))
    tab_s = _rope_table(jnp.full((n_s,), past_len))

    tm = _tile(t, ROW_TILE)
    xp = x_prompt.reshape(b * t, D_MODEL)
    z, xbc, cq, ckv, kpe, kpad, dt = _in_proj(xp, w1, tab_p, dtb, qg, kg, tm)
    y_ssd, h_fin = _ssd_prompt(xbc, z, dt, cw, cb, alog, dsk, nw, expand, tril, b, t)
    tq = _tile(tm, FLASH_TILE)
    q, k, vt = _qkv_prompt(cq, ckv, kpad, tab_p, wq, wuk, wuv, b, t, tm, tq)
    y_mla = _flash_prompt(q, k, vt, tq).reshape(b * t, MLA_HEADS * V_HEAD_DIM)
    h1 = _wo_ln(y_ssd, y_mla, xp, wo, g1, b1, tm)
    tf = FF_TILE
    tm_ffn = _tile(t, FFN_ROW_TILE)
    y_p, g_last = _ffn_prompt(h1, wf_in, wf_down, fcw, fcb, g2, b2, b, t, tm_ffn, tf)

    xs_ = x_sample.reshape(n_s, D_MODEL)
    z_s, xbc_s, cq_s, ckv_s, kpe_s, kpad_s, dt_s = _in_proj(xs_, w1, tab_s, dtb, qg, kg, n_s)
    sc = jnp.transpose(state_conv[0], (1, 0, 2))
    y_ssd_s, st_new = _ssd_step(xbc_s, sc, z_s, dt_s, cw, cb, alog, dsk, nw, expand,
                                state_ssm[0].reshape(n_s, SSM_WIDTH, SSM_STATE))
    ql, qp = _q_step(cq_s, tab_s, wq, wukt)
    pad_heads = lambda a: jnp.pad(jnp.transpose(a, (1, 0, 2)), ((0, 0), (0, HEAD_PAD - MLA_HEADS), (0, 0)))
    o_lat = _decode_attention(page_table, pad_heads(ql), pad_heads(qp), ckv_s.reshape(n_s, 1, KV_LORA),
                              kpad_s.astype(F32).reshape(n_s, 1, LANES), cache_ckv[0],
                              jnp.swapaxes(cache_kpe[0], 1, 2))
    y_mla_s = _uv_step(jnp.transpose(o_lat[:, :MLA_HEADS], (1, 0, 2)).astype(BF16), wuv_h)
    h1_s = _wo_ln(y_ssd_s, y_mla_s, xs_, wo, g1, b1, n_s)
    fbuf = state_ffn_conv[0]
    y_s, g_s = _ffn_step(h1_s, wf_in, wf_down, fcw, fcb, fbuf[:, 0], fbuf[:, 1], g2, b2, tf)

    lead = lambda a: a[None]
    return (y_p.reshape(b, t, D_MODEL),
            y_s.reshape(n_s, 1, D_MODEL),
            lead(ckv.reshape(b, t, KV_LORA)),
            lead(kpe.reshape(b, t, QK_ROPE)),
            lead(h_fin.reshape(b, SSM_HEADS, SSM_HEAD_DIM, SSM_STATE)),
            lead(xbc.reshape(b, t, CONV_DIM)[:, t - (SSM_CONV - 1):]),
            lead(g_last.reshape(b, t // tm_ffn, SUBLANES, D_FF)[:, -1, SUBLANES - (FFN_CONV - 1):]),
            lead(ckv_s.reshape(n_s, 1, KV_LORA)),
            lead(kpe_s.reshape(n_s, 1, QK_ROPE)),
            lead(st_new.reshape(n_s, SSM_HEADS, SSM_HEAD_DIM, SSM_STATE)),
            lead(jnp.concatenate([state_conv[0][:, 1:], xbc_s[:, None]], axis=1)),
            lead(jnp.concatenate([fbuf[:, 1:], g_s[:, None]], axis=1)))
```

```python
import functools
import math

import jax
import jax.numpy as jnp
from jax import lax
from jax.experimental import pallas as pl
from jax.experimental.pallas import tpu as pltpu

F32 = jnp.float32
BF16 = jnp.bfloat16

D_MODEL = 2048
SSM_WIDTH = 1024
SSM_HEAD_DIM = 64
SSM_HEADS = 16
SSM_GROUPS = 2
SSM_STATE = 128
SSM_CONV = 4
SSM_CHUNK = 128
CONV_DIM = SSM_WIDTH + 2 * SSM_GROUPS * SSM_STATE
GROUP_WIDTH = SSM_WIDTH // SSM_GROUPS
MLA_HEADS = 8
V_HEAD_DIM = 128
QK_NOPE = 128
QK_ROPE = 64
Q_LORA = 512
KV_LORA = 512
ROPE_THETA = 10000.0
ATTN_SCALE = (QK_NOPE + QK_ROPE) ** -0.5
D_FF = 5632
FFN_CONV = 3
LN_EPS = 1e-5
RMS_EPS = 1e-6
DEPTH = 1
ALPHA = (2.0 * DEPTH) ** 0.25
OFF_Z = SSM_WIDTH
OFF_XBC = OFF_Z + CONV_DIM
OFF_DT = OFF_XBC + SSM_HEADS
OFF_CQ = OFF_DT + Q_LORA
OFF_CKV = OFF_CQ + KV_LORA

LANES = 128
SUBLANES = 8
QK_PAD = 2 * LANES
HEAD_PAD = 16
VT_ROWS = V_HEAD_DIM + 16
NEG_BIG = -1e30
VMEM_LIMIT = 56 * 1024 * 1024
ROW_TILE = 512
FF_TILE = 512
FFN_ROW_TILE = 512
FFN_PARTS = 2
WO_PARTS = 2
SSD_STEP_SAMPLES = 4
FLASH_HEADS = 4
FLASH_TILE = 512
DECODE_PAGES = 16
DECODE_SLOTS = 3
EXP2_SCALE = ATTN_SCALE * math.log2(math.e)


def _cparams(*sem):
    return pltpu.CompilerParams(dimension_semantics=sem, vmem_limit_bytes=VMEM_LIMIT)


def _dot(a, b):
    return jnp.dot(a, b, preferred_element_type=F32)


def _dot_nt(a, b):
    return lax.dot_general(a, b, (((1,), (1,)), ((), ())), preferred_element_type=F32)


def _split3(x):
    hi = x.astype(BF16)
    r = x - hi.astype(F32)
    mid = r.astype(BF16)
    lo = (r - mid.astype(F32)).astype(BF16)
    return hi, mid, lo


def _dot_exact_lhs(x, sel):
    hi, mid, lo = _split3(x)
    return _dot(hi, sel) + _dot(mid, sel) + _dot(lo, sel)


def _dot_exact_rhs(sel, x):
    hi, mid, lo = _split3(x)
    return _dot(sel, hi) + _dot(sel, mid) + _dot(sel, lo)


def _silu(x):
    hx = 0.5 * x
    return hx + hx * jnp.tanh(hx)


def _softplus(x):
    return jnp.maximum(x, 0.0) + jnp.log1p(jnp.exp(-jnp.abs(x)))


def _rms(x, g):
    r = lax.rsqrt(jnp.mean(x * x, axis=-1, keepdims=True) + RMS_EPS)
    return x * r * g


def _layer_norm(v, g, b):
    mu = jnp.mean(v, axis=-1, keepdims=True)
    d = v - mu
    var = jnp.mean(d * d, axis=-1, keepdims=True)
    return d * lax.rsqrt(var + LN_EPS) * g + b


def _rope_fold(t):
    return t + pltpu.roll(t, QK_ROPE, 1)


def _resident(shape):
    nd = len(shape)
    return pl.BlockSpec(shape, lambda *_: (0,) * nd)


def _in_proj_kernel(x_ref, wssd_ref, wlat_ref, wmisc_ref, tab_ref, dtb_ref, qg_ref, kg_ref,
                    z_ref, xbc_ref, cq_ref, ckv_ref, kpe_ref, kpad_ref, dt_ref):
    xb = x_ref[...].astype(BF16)
    z_ref[...] = _dot(xb, wssd_ref[:, :SSM_WIDTH])
    xbc_ref[...] = _dot(xb, wssd_ref[:, SSM_WIDTH:])
    cq_ref[...] = _rms(_dot(xb, wlat_ref[:, :Q_LORA]), qg_ref[...]).astype(BF16)
    ckv_ref[...] = _rms(_dot(xb, wlat_ref[:, Q_LORA:]), kg_ref[...])
    kr = _rope_fold(_dot(xb, wmisc_ref[:, :LANES]) * tab_ref[...])
    kpe_ref[...] = kr[:, :QK_ROPE]
    lane = lax.broadcasted_iota(jnp.int32, kr.shape, 1)
    kpad_ref[...] = jnp.where(lane < QK_ROPE, kr, 0.0).astype(BF16)
    dt_ref[...] = _softplus(_dot(xb, wmisc_ref[:, LANES:]) + dtb_ref[...])


def _in_proj(x, w1, tab, dtb, qg, kg, tm):
    n = x.shape[0]
    n_tab = tab.shape[0] // tm
    row = lambda w: pl.BlockSpec((tm, w), lambda i: (i, 0))
    return pl.pallas_call(
        _in_proj_kernel,
        grid=(n // tm,),
        in_specs=[row(D_MODEL)] + [_resident(w.shape) for w in w1] + [
                  pl.BlockSpec((tm, LANES), lambda i: (i % n_tab, 0)),
                  _resident(dtb.shape), _resident(qg.shape), _resident(kg.shape)],
        out_specs=[row(SSM_WIDTH), row(CONV_DIM), row(Q_LORA), row(KV_LORA), row(QK_ROPE), row(LANES),
                   row(LANES)],
        out_shape=[jax.ShapeDtypeStruct((n, SSM_WIDTH), F32), jax.ShapeDtypeStruct((n, CONV_DIM), F32),
                   jax.ShapeDtypeStruct((n, Q_LORA), BF16), jax.ShapeDtypeStruct((n, KV_LORA), F32),
                   jax.ShapeDtypeStruct((n, QK_ROPE), F32), jax.ShapeDtypeStruct((n, LANES), BF16),
                   jax.ShapeDtypeStruct((n, LANES), F32)],
        compiler_params=_cparams("arbitrary"),
        name="in_proj",
    )(x, *w1, tab, dtb, qg, kg)


def _gate_and_norm(y, z, nw):
    yg = y * _silu(z)
    parts = []
    for g in range(SSM_GROUPS):
        v = yg[:, g * GROUP_WIDTH:(g + 1) * GROUP_WIDTH]
        parts.append(v * lax.rsqrt(jnp.mean(v * v, axis=-1, keepdims=True) + RMS_EPS))
    return jnp.concatenate(parts, axis=1) * nw


def _ssd_kernel(xbc_ref, z_ref, dt_ref, cw_ref, cb_ref, alog_ref, dsk_ref, nw_ref, e_ref, tril_ref,
                y_ref, hout_ref, ext_ref, ht_ref):
    c = pl.program_id(1)
    L = SSM_CHUNK

    @pl.when(c == 0)
    def _():
        ext_ref[0:SUBLANES, :] = jnp.zeros((SUBLANES, CONV_DIM), F32)
        ht_ref[...] = jnp.zeros_like(ht_ref)

    ext_ref[SUBLANES:SUBLANES + L, :] = xbc_ref[...]
    conv = cb_ref[...] + cw_ref[SSM_CONV - 1:SSM_CONV, :] * xbc_ref[...]
    for k in range(1, SSM_CONV):
        conv = conv + cw_ref[SSM_CONV - 1 - k:SSM_CONV - k, :] * ext_ref[pl.ds(SUBLANES - k, L), :]
    ext_ref[0:SUBLANES, :] = ext_ref[L:L + SUBLANES, :]
    xc = _silu(conv)
    xs = xc[:, :SSM_WIDTH]
    bm = xc[:, SSM_WIDTH:SSM_WIDTH + SSM_GROUPS * SSM_STATE]
    cm = xc[:, SSM_WIDTH + SSM_GROUPS * SSM_STATE:]

    dt = dt_ref[...]
    da = dt * (-jnp.exp(alog_ref[...]))
    acs = _dot_exact_rhs(tril_ref[...], da)
    e = e_ref[...]
    acs_x = _dot_exact_lhs(acs, e)
    dt_x = _dot_exact_lhs(dt, e)
    last_x = acs_x[L - 1:L, :]
    xw = xs * dt_x * jnp.exp(last_x - acs_x)
    exp_acs = jnp.exp(acs_x)
    acs_t = acs.T
    dt_t = dt.T

    row = lax.broadcasted_iota(jnp.int32, (L, L), 0)
    col = lax.broadcasted_iota(jnp.int32, (L, L), 1)
    causal = row >= col
    lower_half = col < SSM_HEAD_DIM

    hprev = ht_ref[...].astype(BF16)
    y_parts = []
    heads_per_group = SSM_HEADS // SSM_GROUPS
    for g in range(SSM_GROUPS):
        gs = slice(g * GROUP_WIDTH, (g + 1) * GROUP_WIDTH)
        bg = bm[:, g * SSM_STATE:(g + 1) * SSM_STATE]
        cg = cm[:, g * SSM_STATE:(g + 1) * SSM_STATE].astype(BF16)
        cb = _dot_nt(cg, bg.astype(BF16))
        y_off = _dot(cg, hprev[:, gs]) * exp_acs[:, gs]
        for jj in range(heads_per_group // 2):
            j = g * (heads_per_group // 2) + jj
            xp = xs[:, j * LANES:(j + 1) * LANES]
            yp = None
            for h, xh in ((2 * j, jnp.where(lower_half, xp, 0.0)), (2 * j + 1, jnp.where(lower_half, 0.0, xp))):
                diff = acs[:, h:h + 1] - acs_t[h:h + 1, :]
                dec = jnp.exp(jnp.where(causal, diff, NEG_BIG))
                m = (cb * dec * dt_t[h:h + 1, :]).astype(BF16)
                t = _dot(m, xh.astype(BF16))
                yp = t if yp is None else yp + t
            y_parts.append(yp + y_off[:, jj * LANES:(jj + 1) * LANES])
        ht_ref[:, gs] = ht_ref[:, gs] * jnp.exp(last_x[:, gs]) + _dot(bg.T.astype(BF16), xw[:, gs].astype(BF16))

    y = jnp.concatenate(y_parts, axis=1) + dsk_ref[...] * xs
    y_ref[...] = _gate_and_norm(y, z_ref[...], nw_ref[...]).astype(BF16)

    @pl.when(c == pl.num_programs(1) - 1)
    def _():
        hout_ref[0] = ht_ref[...].T


def _ssd_prompt(xbc, z, dt, cw, cb, alog, dsk, nw, e, tril, b, t):
    nc = t // SSM_CHUNK
    L = SSM_CHUNK
    row = lambda w: pl.BlockSpec((L, w), lambda bi, c: (bi * nc + c, 0))
    return pl.pallas_call(
        _ssd_kernel,
        grid=(b, nc),
        in_specs=[row(CONV_DIM), row(SSM_WIDTH), row(LANES)] +
                 [_resident(a.shape) for a in (cw, cb, alog, dsk, nw, e, tril)],
        out_specs=[row(SSM_WIDTH), pl.BlockSpec((1, SSM_WIDTH, SSM_STATE), lambda bi, c: (bi, 0, 0))],
        out_shape=[jax.ShapeDtypeStruct((b * t, SSM_WIDTH), BF16),
                   jax.ShapeDtypeStruct((b, SSM_WIDTH, SSM_STATE), F32)],
        scratch_shapes=[pltpu.VMEM((L + SUBLANES, CONV_DIM), F32), pltpu.VMEM((SSM_STATE, SSM_WIDTH), F32)],
        compiler_params=_cparams("arbitrary", "arbitrary"),
        name="ssd_prompt",
    )(xbc, z, dt, cw, cb, alog, dsk, nw, e, tril)


def _ssd_step_kernel(xbc_ref, sc_ref, z_ref, dt_ref, cw_ref, cb_ref, alog_ref, dsk_ref, nw_ref, e_ref, st_ref,
                     y_ref, so_ref, xs_s, b_s, ct_s, xdt_s, da_s, yt_s):
    blk = pl.program_id(0)
    per_step = st_ref.shape[0]

    @pl.when(blk == 0)
    def _():
        conv = cb_ref[...] + cw_ref[SSM_CONV - 1:SSM_CONV, :] * xbc_ref[...]
        for k in range(SSM_CONV - 1):
            conv = conv + cw_ref[k:k + 1, :] * sc_ref[k]
        xc = _silu(conv)
        xs = xc[:, :SSM_WIDTH]
        xs_s[...] = xs
        b_s[...] = xc[:, SSM_WIDTH:SSM_WIDTH + SSM_GROUPS * SSM_STATE]
        ct_s[...] = xc[:, SSM_WIDTH + SSM_GROUPS * SSM_STATE:].T
        dt = dt_ref[...]
        xdt_s[...] = (xs * _dot_exact_lhs(dt, e_ref[...])).T.astype(BF16)
        da_s[...] = jnp.exp(dt * (-jnp.exp(alog_ref[...])))
        yt_s[...] = jnp.zeros_like(yt_s)

    n_samp = xs_s.shape[0]
    row = lax.broadcasted_iota(jnp.int32, (n_samp, SSM_STATE), 0)
    lane = lax.broadcasted_iota(jnp.int32, (SSM_STATE, n_samp), 1)
    heads_per_group = SSM_HEADS // SSM_GROUPS
    for k in range(per_step):
        s = blk * per_step + k
        pick_row = row == s
        decay = jnp.broadcast_to(da_s[pl.ds(s, 1), :], (SSM_STATE, SSM_STATE)).T
        b_row = b_s[pl.ds(s, 1), :]
        for g in range(SSM_GROUPS):
            gs = slice(g * GROUP_WIDTH, (g + 1) * GROUP_WIDTH)
            ns = slice(g * SSM_STATE, (g + 1) * SSM_STATE)
            eb = jnp.where(pick_row, b_row[:, ns], 0.0).astype(BF16)
            upd = _dot(xdt_s[gs, :], eb)
            for hh in range(heads_per_group):
                h = g * heads_per_group + hh
                rs = slice(h * SSM_HEAD_DIM, (h + 1) * SSM_HEAD_DIM)
                so_ref[k, rs, :] = (st_ref[k, rs, :] * decay[h:h + 1, :]
                                    + upd[hh * SSM_HEAD_DIM:(hh + 1) * SSM_HEAD_DIM, :])
            ce = jnp.where(lane == s, ct_s[ns, :], 0.0).astype(BF16)
            yt_s[gs, :] += _dot(so_ref[k, gs, :].astype(BF16), ce)

    @pl.when(blk == pl.num_programs(0) - 1)
    def _():
        y = yt_s[...].T + dsk_ref[...] * xs_s[...]
        y_ref[...] = _gate_and_norm(y, z_ref[...], nw_ref[...]).astype(BF16)


def _ssd_step(xbc, sc, z, dt, cw, cb, alog, dsk, nw, e, state):
    n_s = xbc.shape[0]
    per_step = _tile(n_s, SSD_STEP_SAMPLES)
    st_spec = pl.BlockSpec((per_step, SSM_WIDTH, SSM_STATE), lambda s: (s, 0, 0))
    return pl.pallas_call(
        _ssd_step_kernel,
        grid=(n_s // per_step,),
        in_specs=[_resident(a.shape) for a in (xbc, sc, z, dt, cw, cb, alog, dsk, nw, e)] + [st_spec],
        out_specs=[_resident((n_s, SSM_WIDTH)), st_spec],
        out_shape=[jax.ShapeDtypeStruct((n_s, SSM_WIDTH), BF16),
                   jax.ShapeDtypeStruct((n_s, SSM_WIDTH, SSM_STATE), F32)],
        scratch_shapes=[pltpu.VMEM((n_s, SSM_WIDTH), F32), pltpu.VMEM((n_s, SSM_GROUPS * SSM_STATE), F32),
                        pltpu.VMEM((SSM_GROUPS * SSM_STATE, n_s), F32), pltpu.VMEM((SSM_WIDTH, n_s), BF16),
                        pltpu.VMEM((n_s, LANES), F32), pltpu.VMEM((SSM_WIDTH, n_s), F32)],
        compiler_params=_cparams("arbitrary"),
        name="ssd_step",
    )(xbc, sc, z, dt, cw, cb, alog, dsk, nw, e, state)


def _rope_q_head(qh, tab, lane):
    r = _rope_fold(qh[:, QK_NOPE:] * tab)
    return qh[:, :QK_NOPE], jnp.where(lane < QK_ROPE, r, 0.0)


def _qkv_kernel(cq_ref, ckv_ref, kpad_ref, tab_ref, wq_ref, wuk_ref, wuv_ref, q_ref, k_ref, v_ref):
    q = _dot(cq_ref[...], wq_ref[...])
    ckv = ckv_ref[...].astype(BF16)
    kn = _dot(ckv, wuk_ref[...])
    v = _dot(ckv, wuv_ref[...])
    tab = tab_ref[...]
    kpad = kpad_ref[...]
    lane = lax.broadcasted_iota(jnp.int32, tab.shape, 1)
    for h in range(MLA_HEADS):
        nope, pe = _rope_q_head(q[:, h * QK_PAD:(h + 1) * QK_PAD], tab, lane)
        q_ref[0, h] = (jnp.concatenate([nope, pe], axis=1) * EXP2_SCALE).astype(BF16)
        k_ref[0, h] = jnp.concatenate([kn[:, h * QK_NOPE:(h + 1) * QK_NOPE].astype(BF16), kpad], axis=1)
        vt = v[:, h * V_HEAD_DIM:(h + 1) * V_HEAD_DIM].T.astype(BF16)
        tkv = v_ref.shape[-1]
        extra = lax.broadcasted_iota(jnp.int32, (VT_ROWS - V_HEAD_DIM, tkv), 0)
        ones_row = jnp.where(extra == 0, 1.0, 0.0).astype(BF16)
        for c in range(v_ref.shape[2]):
            v_ref[0, h, c, 0:V_HEAD_DIM, :] = vt[:, c * tkv:(c + 1) * tkv]
            v_ref[0, h, c, V_HEAD_DIM:, :] = ones_row


def _qkv_prompt(cq, ckv, kpad, tab, wq, wuk, wuv, b, t, tm, tkv):
    nt = t // tm
    nc = tm // tkv
    row = lambda w: pl.BlockSpec((tm, w), lambda bi, i: (bi * nt + i, 0))
    head = lambda w: pl.BlockSpec((1, MLA_HEADS, tm, w), lambda bi, i: (bi, 0, i, 0))
    return pl.pallas_call(
        _qkv_kernel,
        grid=(b, nt),
        in_specs=[row(Q_LORA), row(KV_LORA), row(LANES), pl.BlockSpec((tm, LANES), lambda bi, i: (i, 0)),
                  _resident(wq.shape), _resident(wuk.shape), _resident(wuv.shape)],
        out_specs=[head(QK_PAD), head(QK_PAD),
                   pl.BlockSpec((1, MLA_HEADS, nc, VT_ROWS, tkv), lambda bi, i: (bi, 0, i, 0, 0))],
        out_shape=[jax.ShapeDtypeStruct((b, MLA_HEADS, t, QK_PAD), BF16),
                   jax.ShapeDtypeStruct((b, MLA_HEADS, t, QK_PAD), BF16),
                   jax.ShapeDtypeStruct((b, MLA_HEADS, t // tkv, VT_ROWS, tkv), BF16)],
        compiler_params=_cparams("arbitrary", "arbitrary"),
        name="qkv_prompt",
    )(cq, ckv, kpad, tab, wq, wuk, wuv)


def _q_step_kernel(cq_ref, tab_ref, wq_ref, wukt_ref, ql_ref, qp_ref):
    q = _dot(cq_ref[...], wq_ref[...])
    tab = tab_ref[...]
    lane = lax.broadcasted_iota(jnp.int32, tab.shape, 1)
    for h in range(MLA_HEADS):
        nope, pe = _rope_q_head(q[:, h * QK_PAD:(h + 1) * QK_PAD], tab, lane)
        ql_ref[h] = _dot(nope.astype(BF16), wukt_ref[h]).astype(BF16)
        qp_ref[h] = pe.astype(BF16)


def _q_step(cq, tab, wq, wukt):
    n_s = cq.shape[0]
    return pl.pallas_call(
        _q_step_kernel,
        grid=(1,),
        in_specs=[_resident(a.shape) for a in (cq, tab, wq, wukt)],
        out_specs=[_resident((MLA_HEADS, n_s, KV_LORA)), _resident((MLA_HEADS, n_s, LANES))],
        out_shape=[jax.ShapeDtypeStruct((MLA_HEADS, n_s, KV_LORA), BF16),
                   jax.ShapeDtypeStruct((MLA_HEADS, n_s, LANES), BF16)],
        compiler_params=_cparams("arbitrary"),
        name="q_step",
    )(cq, tab, wq, wukt)


def _softmax_update(s, m_ref, l_ref):
    m_prev = m_ref[...]
    m_new = jnp.maximum(m_prev, jnp.max(s, axis=-1, keepdims=True))
    corr = jnp.exp2((m_prev - m_new) * EXP2_SCALE)
    p = jnp.exp2((s - m_new) * EXP2_SCALE)
    l_ref[...] = l_ref[...] * corr + jnp.sum(p, axis=-1, keepdims=True)
    m_ref[...] = m_new
    return p, corr


def _flash_kernel(q_ref, k_ref, vt_ref, o_ref, *scratch, tq, nh):
    m_s, acc_s = scratch[:nh], scratch[nh:]
    qi = pl.program_id(2)
    for h in range(nh):
        m_s[h][...] = jnp.full_like(m_s[h], NEG_BIG)
        acc_s[h][...] = jnp.zeros_like(acc_s[h])

    def step(j, masked):
        start = pl.multiple_of(j * tq, tq)
        sts = [_dot_nt(k_ref[0, h, pl.ds(start, tq), :], q_ref[0, h]) for h in range(nh)]
        ps, corrs = [], []
        for h in range(nh):
            st = sts[h]
            if masked:
                key = lax.broadcasted_iota(jnp.int32, st.shape, 0)
                qry = lax.broadcasted_iota(jnp.int32, st.shape, 1)
                st = jnp.where(key <= qry, st, NEG_BIG)
            m_prev = m_s[h][...]
            m_new = jnp.maximum(m_prev, jnp.max(st, axis=0, keepdims=True))
            corr = jnp.exp2(m_prev - m_new)
            m_s[h][...] = m_new
            ps.append(jnp.exp2(st - m_new).astype(BF16))
            corrs.append(corr)
        for h in range(nh):
            acc_s[h][...] = acc_s[h][...] * corrs[h] + _dot(vt_ref[0, h, j], ps[h])

    def body(j, carry):
        step(j, False)
        return carry

    lax.fori_loop(0, qi, body, 0)
    step(qi, True)
    for h in range(nh):
        o = acc_s[h][0:V_HEAD_DIM, :] / acc_s[h][V_HEAD_DIM:V_HEAD_DIM + 1, :]
        o_ref[0, :, h * V_HEAD_DIM:(h + 1) * V_HEAD_DIM] = o.T.astype(BF16)


def _flash_prompt(q, k, vt, tq):
    b, h, t, _ = q.shape
    nh = FLASH_HEADS
    return pl.pallas_call(
        functools.partial(_flash_kernel, tq=tq, nh=nh),
        grid=(b, h // nh, t // tq),
        in_specs=[pl.BlockSpec((1, nh, tq, QK_PAD), lambda bi, hi, qi: (bi, hi, qi, 0)),
                  pl.BlockSpec((1, nh, t, QK_PAD), lambda bi, hi, qi: (bi, hi, 0, 0)),
                  pl.BlockSpec((1, nh, t // tq, VT_ROWS, tq), lambda bi, hi, qi: (bi, hi, 0, 0, 0))],
        out_specs=pl.BlockSpec((1, tq, nh * V_HEAD_DIM), lambda bi, hi, qi: (bi, qi, hi)),
        out_shape=jax.ShapeDtypeStruct((b, t, h * V_HEAD_DIM), BF16),
        scratch_shapes=([pltpu.VMEM((1, tq), F32)] * nh + [pltpu.VMEM((VT_ROWS, tq), F32)] * nh),
        compiler_params=_cparams("arbitrary", "arbitrary", "arbitrary"),
        name="flash_prompt",
    )(q, k, vt)


def _decode_kernel(pt_ref, ql_ref, qp_ref, cn_ref, kn_ref, ckv_hbm, kpe_hbm, o_ref,
                   kc_buf, kp_buf, sem, m_s, l_s, acc_s, *, n_pg, n_groups, page):
    s = pl.program_id(0)
    n_slots = kc_buf.shape[0]
    ahead = n_slots - 1
    total = pl.num_programs(0) * n_groups

    def copies(samp, grp, slot):
        out = []
        for i in range(n_pg):
            pg = pt_ref[samp, grp * n_pg + i]
            out.append(pltpu.make_async_copy(ckv_hbm.at[pg], kc_buf.at[slot, pl.ds(i * page, page), :],
                                             sem.at[0, slot]))
            out.append(pltpu.make_async_copy(kpe_hbm.at[pg], kp_buf.at[slot, i], sem.at[1, slot]))
        return out

    def start(samp, grp, slot):
        for c in copies(samp, grp, slot):
            c.start()

    @pl.when(s == 0)
    def _():
        for t in range(ahead):
            start(t // n_groups, t % n_groups, t % n_slots)

    ql = ql_ref[0]
    qp = qp_ref[0][:, :QK_ROPE]
    cn = cn_ref[...].astype(BF16).astype(F32)
    kn = kn_ref[...].astype(BF16).astype(F32)
    m_s[...] = (jnp.sum(ql.astype(F32) * cn, axis=-1, keepdims=True)
                + jnp.sum(qp_ref[0].astype(F32) * kn, axis=-1, keepdims=True))
    l_s[...] = jnp.ones_like(l_s)
    acc_s[...] = jnp.broadcast_to(cn, acc_s.shape)

    def group(g, carry):
        t = s * n_groups + g
        slot = lax.rem(t, n_slots)
        g_next = g + ahead
        wraps = g_next >= n_groups

        @pl.when(t + ahead < total)
        def _():
            start(jnp.where(wraps, s + 1, s), jnp.where(wraps, g_next - n_groups, g_next),
                  lax.rem(t + ahead, n_slots))

        for c in copies(s, g, slot):
            c.wait()
        kc = kc_buf[slot].astype(BF16)
        s_pe = [_dot(qp, kp_buf[slot, i].astype(BF16)) for i in range(n_pg)]
        sc = _dot_nt(ql, kc) + jnp.concatenate(s_pe, axis=1)
        p, corr = _softmax_update(sc, m_s, l_s)
        acc_s[...] = acc_s[...] * corr + _dot(p.astype(BF16), kc)
        return carry

    lax.fori_loop(0, n_groups, group, 0)
    o_ref[0] = acc_s[...] / l_s[...]


def _decode_attention(page_table, ql, qp, ckv_new, kpe_new, ckv_pool, kpe_pool_t):
    n_s, n_pages = page_table.shape
    page = ckv_pool.shape[1]
    n_pg = min(DECODE_PAGES, n_pages // 2)
    n_slots = DECODE_SLOTS
    assert n_pages % n_pg == 0 and n_slots - 1 <= n_pages // n_pg
    grid_spec = pltpu.PrefetchScalarGridSpec(
        num_scalar_prefetch=1,
        grid=(n_s,),
        in_specs=[pl.BlockSpec((1, HEAD_PAD, KV_LORA), lambda s, pt: (s, 0, 0)),
                  pl.BlockSpec((1, HEAD_PAD, LANES), lambda s, pt: (s, 0, 0)),
                  pl.BlockSpec((None, 1, KV_LORA), lambda s, pt: (s, 0, 0)),
                  pl.BlockSpec((None, 1, LANES), lambda s, pt: (s, 0, 0)),
                  pl.BlockSpec(memory_space=pl.ANY), pl.BlockSpec(memory_space=pl.ANY)],
        out_specs=pl.BlockSpec((1, HEAD_PAD, KV_LORA), lambda s, pt: (s, 0, 0)),
        scratch_shapes=[pltpu.VMEM((n_slots, n_pg * page, KV_LORA), F32),
                        pltpu.VMEM((n_slots, n_pg, QK_ROPE, page), F32),
                        pltpu.SemaphoreType.DMA((2, n_slots)),
                        pltpu.VMEM((HEAD_PAD, 1), F32), pltpu.VMEM((HEAD_PAD, 1), F32),
                        pltpu.VMEM((HEAD_PAD, KV_LORA), F32)],
    )
    return pl.pallas_call(
        functools.partial(_decode_kernel, n_pg=n_pg, n_groups=n_pages // n_pg, page=page),
        grid_spec=grid_spec,
        out_shape=jax.ShapeDtypeStruct((n_s, HEAD_PAD, KV_LORA), F32),
        compiler_params=_cparams("arbitrary"),
        name="decode_attention",
    )(page_table, ql, qp, ckv_new, kpe_new, ckv_pool, kpe_pool_t)


def _uv_kernel(o_ref, wuv_ref, y_ref):
    for h in range(MLA_HEADS):
        y_ref[:, h * V_HEAD_DIM:(h + 1) * V_HEAD_DIM] = _dot(o_ref[h], wuv_ref[h]).astype(BF16)


def _uv_step(o, wuv):
    n_s = o.shape[1]
    return pl.pallas_call(
        _uv_kernel,
        grid=(1,),
        in_specs=[_resident(o.shape), _resident(wuv.shape)],
        out_specs=_resident((n_s, MLA_HEADS * V_HEAD_DIM)),
        out_shape=jax.ShapeDtypeStruct((n_s, MLA_HEADS * V_HEAD_DIM), BF16),
        compiler_params=_cparams("arbitrary"),
        name="uv_step",
    )(o, wuv)


def _wo_ln_kernel(ys_ref, ym_ref, x_ref, wo_ref, g_ref, b_ref, h_ref, hb_ref):
    rows = x_ref.shape[0] // WO_PARTS
    parts = [slice(r * rows, (r + 1) * rows) for r in range(WO_PARTS)]
    mixes = [_dot(ys_ref[rs, :], wo_ref[0:SSM_WIDTH, :]) + _dot(ym_ref[rs, :], wo_ref[SSM_WIDTH:, :])
             for rs in parts]
    for rs, mix in zip(parts, mixes):
        h = _layer_norm(ALPHA * x_ref[rs, :] + mix, g_ref[...], b_ref[...])
        h_ref[rs, :] = h
        hb_ref[rs, :] = h.astype(BF16)


def _wo_ln(ys, ym, x, wo, g, b, tm):
    n = x.shape[0]
    row = lambda w: pl.BlockSpec((tm, w), lambda i: (i, 0))
    return pl.pallas_call(
        _wo_ln_kernel,
        grid=(n // tm,),
        in_specs=[row(SSM_WIDTH), row(SSM_WIDTH), row(D_MODEL), _resident(wo.shape), _resident(g.shape),
                  _resident(b.shape)],
        out_specs=[row(D_MODEL), row(D_MODEL)],
        out_shape=[jax.ShapeDtypeStruct((n, D_MODEL), F32), jax.ShapeDtypeStruct((n, D_MODEL), BF16)],
        compiler_params=_cparams("arbitrary"),
        name="wo_ln",
    )(ys, ym, x, wo, g, b)


def _ffn_begin(j, acc_s):
    @pl.when(j == 0)
    def _():
        acc_s[...] = jnp.zeros_like(acc_s)


def _ffn_gate_up(hb_ref, wg_ref, wu_ref):
    hb = hb_ref[...]
    w = wg_ref.shape[1] // FFN_PARTS
    parts = [slice(c * w, (c + 1) * w) for c in range(FFN_PARTS)]
    return [(cs, _dot(hb, wg_ref[:, cs]), _dot(hb, wu_ref[:, cs])) for cs in parts]


def _ffn_end(j, h_ref, g2_ref, b2_ref, y_ref, acc_s):
    @pl.when(j == pl.num_programs(1) - 1)
    def _():
        y_ref[...] = _layer_norm(ALPHA * h_ref[...] + acc_s[...], g2_ref[...], b2_ref[...])


def _ffn_prompt_kernel(h_ref, hb_ref, wg_ref, wu_ref, wd_ref, cw_ref, cb_ref, g2_ref, b2_ref, y_ref, gl_ref,
                       acc_s, ext_s, carry_s, *, blocks_per_seq):
    i = pl.program_id(0)
    j = pl.program_id(1)
    tm = h_ref.shape[0]
    _ffn_begin(j, acc_s)

    @pl.when(i % blocks_per_seq == 0)
    def _():
        carry_s[j] = jnp.zeros(carry_s.shape[1:], F32)

    for cs, g, u in _ffn_gate_up(hb_ref, wg_ref, wu_ref):
        ext_s[0:SUBLANES, cs] = carry_s[j, :, cs]
        ext_s[SUBLANES:, cs] = g
        gc = cb_ref[:, cs] + cw_ref[FFN_CONV - 1:FFN_CONV, cs] * g
        for k in range(1, FFN_CONV):
            gc = gc + cw_ref[FFN_CONV - 1 - k:FFN_CONV - k, cs] * ext_s[pl.ds(SUBLANES - k, tm), cs]
        tail = ext_s[tm:tm + SUBLANES, cs]
        carry_s[j, :, cs] = tail
        gl_ref[0, :, cs] = tail
        acc_s[...] += _dot((_silu(gc) * u).astype(BF16), wd_ref[cs, :])
    _ffn_end(j, h_ref, g2_ref, b2_ref, y_ref, acc_s)


def _ffn_step_kernel(h_ref, hb_ref, wg_ref, wu_ref, wd_ref, cw_ref, cb_ref, p2_ref, p1_ref, g2_ref, b2_ref,
                     y_ref, gout_ref, acc_s):
    j = pl.program_id(1)
    _ffn_begin(j, acc_s)
    for cs, g, u in _ffn_gate_up(hb_ref, wg_ref, wu_ref):
        gout_ref[:, cs] = g
        gc = (cb_ref[:, cs] + cw_ref[0:1, cs] * p2_ref[:, cs] + cw_ref[1:2, cs] * p1_ref[:, cs]
              + cw_ref[2:3, cs] * g)
        acc_s[...] += _dot((_silu(gc) * u).astype(BF16), wd_ref[cs, :])
    _ffn_end(j, h_ref, g2_ref, b2_ref, y_ref, acc_s)


def _ffn_specs(tm, tf):
    return [pl.BlockSpec((tm, D_MODEL), lambda i, j: (i, 0)),
            pl.BlockSpec((tm, D_MODEL), lambda i, j: (i, 0)),
            pl.BlockSpec((D_MODEL, tf), lambda i, j: (0, j)),
            pl.BlockSpec((D_MODEL, tf), lambda i, j: (0, j + D_FF // tf)),
            pl.BlockSpec((tf, D_MODEL), lambda i, j: (j, 0)),
            pl.BlockSpec((FFN_CONV, tf), lambda i, j: (0, j)),
            pl.BlockSpec((1, tf), lambda i, j: (0, j))]


def _ffn_prompt(h, hb, w_in, w_down, cw, cb, g2, b2, b, t, tm, tf):
    n = h.shape[0]
    nj = D_FF // tf
    bps = t // tm
    vec = pl.BlockSpec((1, D_MODEL), lambda i, j: (0, 0))
    return pl.pallas_call(
        functools.partial(_ffn_prompt_kernel, blocks_per_seq=bps),
        grid=(n // tm, nj),
        in_specs=_ffn_specs(tm, tf) + [vec, vec],
        out_specs=[pl.BlockSpec((tm, D_MODEL), lambda i, j: (i, 0)),
                   pl.BlockSpec((1, SUBLANES, tf), lambda i, j: (i, 0, j))],
        out_shape=[jax.ShapeDtypeStruct((n, D_MODEL), F32), jax.ShapeDtypeStruct((n // tm, SUBLANES, D_FF), F32)],
        scratch_shapes=[pltpu.VMEM((tm, D_MODEL), F32),
                        pltpu.VMEM((tm + SUBLANES, tf), F32), pltpu.VMEM((nj, SUBLANES, tf), F32)],
        compiler_params=_cparams("arbitrary", "arbitrary"),
        name="ffn_prompt",
    )(h, hb, w_in, w_in, w_down, cw, cb, g2, b2)


def _ffn_step(h, hb, w_in, w_down, cw, cb, p2, p1, g2, b2, tf):
    n = h.shape[0]
    nj = D_FF // tf
    vec = pl.BlockSpec((1, D_MODEL), lambda i, j: (0, 0))
    col = pl.BlockSpec((n, tf), lambda i, j: (0, j))
    return pl.pallas_call(
        _ffn_step_kernel,
        grid=(1, nj),
        in_specs=_ffn_specs(n, tf) + [col, col, vec, vec],
        out_specs=[pl.BlockSpec((n, D_MODEL), lambda i, j: (0, 0)), col],
        out_shape=[jax.ShapeDtypeStruct((n, D_MODEL), F32), jax.ShapeDtypeStruct((n, D_FF), F32)],
        scratch_shapes=[pltpu.VMEM((n, D_MODEL), F32)],
        compiler_params=_cparams("arbitrary", "arbitrary"),
        name="ffn_step",
    )(h, hb, w_in, w_in, w_down, cw, cb, p2, p1, g2, b2)


def _rope_table(pos):
    inv_freq = ROPE_THETA ** (-jnp.arange(0, QK_ROPE, 2, dtype=F32) / QK_ROPE)
    ang = pos.astype(F32)[:, None] * inv_freq[None, :]
    c, s = jnp.cos(ang), jnp.sin(ang)
    return jnp.concatenate([c, c, -s, s], axis=1)


def _swap_halves(w):
    half = w.shape[-1] // 2
    return jnp.concatenate([w[..., half:], w[..., :half]], axis=-1)


def _pad_lanes(v):
    return jnp.pad(v.reshape(1, -1).astype(F32), ((0, 0), (0, LANES - v.shape[-1])))


def _tile(n, cap):
    t = min(n, cap)
    assert n % t == 0
    return t


def kernel(x_prompt, x_sample, cache_ckv, cache_kpe, page_table, state_ssm, state_conv, state_ffn_conv,
           w_in, conv_w, conv_b, dt_bias, a_log, d_skip, ssm_norm_w, q_norm_w, kv_norm_w,
           w_uq, w_uk, w_uv, w_o, ln1_g, ln1_b, w_ffn_in, ffn_conv_w, ffn_conv_b, w_ffn_down, ln2_g, ln2_b):
    assert w_in.shape[0] == DEPTH == 1 and x_sample.shape[1] == 1
    b, t, _ = x_prompt.shape
    n_s = x_sample.shape[0]
    n_pages = page_table.shape[1]
    past_len = n_pages * cache_ckv.shape[2]
    assert t % SSM_CHUNK == 0

    wi = w_in[0]
    w_kpe = wi[:, OFF_CKV:]
    w_misc = jnp.concatenate([w_kpe, _swap_halves(w_kpe), wi[:, OFF_XBC:OFF_DT],
                              jnp.zeros((D_MODEL, LANES - SSM_HEADS), F32)], axis=1)
    w1 = (wi[:, :OFF_XBC].astype(BF16), wi[:, OFF_DT:OFF_CKV].astype(BF16), w_misc.astype(BF16))
    uq = w_uq[0]
    uq_pe = uq[:, :, QK_NOPE:]
    wq = jnp.concatenate([uq[:, :, :QK_NOPE], uq_pe, _swap_halves(uq_pe)], axis=-1)
    wq = wq.reshape(Q_LORA, MLA_HEADS * QK_PAD).astype(BF16)
    wuk = w_uk[0].reshape(KV_LORA, MLA_HEADS * QK_NOPE).astype(BF16)
    wuv = w_uv[0].reshape(KV_LORA, MLA_HEADS * V_HEAD_DIM).astype(BF16)
    wukt = jnp.transpose(w_uk[0], (1, 2, 0)).astype(BF16)
    wuv_h = jnp.transpose(w_uv[0], (1, 0, 2)).astype(BF16)
    wo = w_o[0].astype(BF16)
    wf_in = w_ffn_in[0].astype(BF16)
    wf_down = w_ffn_down[0].astype(BF16)
    row = lambda v: v.reshape(1, -1).astype(F32)
    dtb, alog = _pad_lanes(dt_bias[0]), _pad_lanes(a_log[0])
    dsk = row(jnp.repeat(d_skip[0], SSM_HEAD_DIM))
    nw, qg, kg = row(ssm_norm_w[0]), row(q_norm_w[0]), row(kv_norm_w[0])
    cw, cb = conv_w[0], row(conv_b[0])
    fcw, fcb = ffn_conv_w[0], row(ffn_conv_b[0])
    g1, b1, g2, b2 = row(ln1_g[0]), row(ln1_b[0]), row(ln2_g[0]), row(ln2_b[0])
    expand = (jnp.arange(SSM_WIDTH)[None, :] // SSM_HEAD_DIM == jnp.arange(LANES)[:, None]).astype(BF16)
    tril = (jnp.arange(SSM_CHUNK)[:, None] >= jnp.arange(SSM_CHUNK)[None, :]).astype(BF16)
    tab_p = _rope_table(jnp.arange(t))
    tab_s = _rope_table(jnp.full((n_s,), past_len))

    tm = _tile(t, ROW_TILE)
    xp = x_prompt.reshape(b * t, D_MODEL)
    z, xbc, cq, ckv, kpe, kpad, dt = _in_proj(xp, w1, tab_p, dtb, qg, kg, tm)
    y_ssd, h_fin = _ssd_prompt(xbc, z, dt, cw, cb, alog, dsk, nw, expand, tril, b, t)
    tq = _tile(tm, FLASH_TILE)
    q, k, vt = _qkv_prompt(cq, ckv, kpad, tab_p, wq, wuk, wuv, b, t, tm, tq)
    y_mla = _flash_prompt(q, k, vt, tq).reshape(b * t, MLA_HEADS * V_HEAD_DIM)
    h1, h1b = _wo_ln(y_ssd, y_mla, xp, wo, g1, b1, tm)
    tf = FF_TILE
    tm_ffn = _tile(t, FFN_ROW_TILE)
    y_p, g_last = _ffn_prompt(h1, h1b, wf_in, wf_down, fcw, fcb, g2, b2, b, t, tm_ffn, tf)

    xs_ = x_sample.reshape(n_s, D_MODEL)
    z_s, xbc_s, cq_s, ckv_s, kpe_s, kpad_s, dt_s = _in_proj(xs_, w1, tab_s, dtb, qg, kg, n_s)
    sc = jnp.transpose(state_conv[0], (1, 0, 2))
    y_ssd_s, st_new = _ssd_step(xbc_s, sc, z_s, dt_s, cw, cb, alog, dsk, nw, expand,
                                state_ssm[0].reshape(n_s, SSM_WIDTH, SSM_STATE))
    ql, qp = _q_step(cq_s, tab_s, wq, wukt)
    pad_heads = lambda a: jnp.pad(jnp.transpose(a, (1, 0, 2)), ((0, 0), (0, HEAD_PAD - MLA_HEADS), (0, 0)))
    o_lat = _decode_attention(page_table, pad_heads(ql), pad_heads(qp), ckv_s.reshape(n_s, 1, KV_LORA),
                              kpad_s.astype(F32).reshape(n_s, 1, LANES), cache_ckv[0],
                              jnp.swapaxes(cache_kpe[0], 1, 2))
    y_mla_s = _uv_step(jnp.transpose(o_lat[:, :MLA_HEADS], (1, 0, 2)).astype(BF16), wuv_h)
    h1_s, h1b_s = _wo_ln(y_ssd_s, y_mla_s, xs_, wo, g1, b1, n_s)
    fbuf = state_ffn_conv[0]
    y_s, g_s = _ffn_step(h1_s, h1b_s, wf_in, wf_down, fcw, fcb, fbuf[:, 0], fbuf[:, 1], g2, b2, tf)

    lead = lambda a: a[None]
    return (y_p.reshape(b, t, D_MODEL),
            y_s.reshape(n_s, 1, D_MODEL),
            lead(ckv.reshape(b, t, KV_LORA)),
            lead(kpe.reshape(b, t, QK_ROPE)),
            lead(h_fin.reshape(b, SSM_HEADS, SSM_HEAD_DIM, SSM_STATE)),
            lead(xbc.reshape(b, t, CONV_DIM)[:, t - (SSM_CONV - 1):]),
            lead(g_last.reshape(b, t // tm_ffn, SUBLANES, D_FF)[:, -1, SUBLANES - (FFN_CONV - 1):]),
            lead(ckv_s.reshape(n_s, 1, KV_LORA)),
            lead(kpe_s.reshape(n_s, 1, QK_ROPE)),
            lead(st_new.reshape(n_s, SSM_HEADS, SSM_HEAD_DIM, SSM_STATE)),
            lead(jnp.concatenate([state_conv[0][:, 1:], xbc_s[:, None]], axis=1)),
            lead(jnp.concatenate([fbuf[:, 1:], g_s[:, None]], axis=1)))
```

```python
import functools
import math

import jax
import jax.numpy as jnp
from jax import lax
from jax.experimental import pallas as pl
from jax.experimental.pallas import tpu as pltpu

F32 = jnp.float32
BF16 = jnp.bfloat16

D_MODEL = 2048
SSM_WIDTH = 1024
SSM_HEAD_DIM = 64
SSM_HEADS = 16
SSM_GROUPS = 2
SSM_STATE = 128
SSM_CONV = 4
SSM_CHUNK = 128
CONV_DIM = SSM_WIDTH + 2 * SSM_GROUPS * SSM_STATE
GROUP_WIDTH = SSM_WIDTH // SSM_GROUPS
MLA_HEADS = 8
V_HEAD_DIM = 128
QK_NOPE = 128
QK_ROPE = 64
Q_LORA = 512
KV_LORA = 512
ROPE_THETA = 10000.0
ATTN_SCALE = (QK_NOPE + QK_ROPE) ** -0.5
D_FF = 5632
FFN_CONV = 3
LN_EPS = 1e-5
RMS_EPS = 1e-6
DEPTH = 1
ALPHA = (2.0 * DEPTH) ** 0.25
OFF_Z = SSM_WIDTH
OFF_XBC = OFF_Z + CONV_DIM
OFF_DT = OFF_XBC + SSM_HEADS
OFF_CQ = OFF_DT + Q_LORA
OFF_CKV = OFF_CQ + KV_LORA

LANES = 128
SUBLANES = 8
QK_PAD = 2 * LANES
HEAD_PAD = 16
VT_ROWS = V_HEAD_DIM + 16
NEG_BIG = -1e30
VMEM_LIMIT = 56 * 1024 * 1024
ROW_TILE = 512
FF_TILE = 512
FFN_ROW_TILE = 512
FFN_PARTS = 2
WO_PARTS = 2
SSD_STEP_SAMPLES = 4
FLASH_HEADS = 4
FLASH_TILE = 512
DECODE_PAGES = 16
DECODE_SLOTS = 3
EXP2_SCALE = ATTN_SCALE * math.log2(math.e)


def _cparams(*sem):
    return pltpu.CompilerParams(dimension_semantics=sem, vmem_limit_bytes=VMEM_LIMIT)


def _dot(a, b):
    return jnp.dot(a, b, preferred_element_type=F32)


def _dot_nt(a, b):
    return lax.dot_general(a, b, (((1,), (1,)), ((), ())), preferred_element_type=F32)


def _split3(x):
    hi = x.astype(BF16)
    r = x - hi.astype(F32)
    mid = r.astype(BF16)
    lo = (r - mid.astype(F32)).astype(BF16)
    return hi, mid, lo


def _dot_exact_lhs(x, sel):
    hi, mid, lo = _split3(x)
    return _dot(hi, sel) + _dot(mid, sel) + _dot(lo, sel)


def _dot_exact_rhs(sel, x):
    hi, mid, lo = _split3(x)
    return _dot(sel, hi) + _dot(sel, mid) + _dot(sel, lo)


def _silu(x):
    hx = 0.5 * x
    return hx + hx * jnp.tanh(hx)


def _softplus(x):
    return jnp.maximum(x, 0.0) + jnp.log1p(jnp.exp(-jnp.abs(x)))


def _rms(x, g):
    r = lax.rsqrt(jnp.mean(x * x, axis=-1, keepdims=True) + RMS_EPS)
    return x * r * g


def _layer_norm(v, g, b):
    mu = jnp.mean(v, axis=-1, keepdims=True)
    d = v - mu
    var = jnp.mean(d * d, axis=-1, keepdims=True)
    return d * lax.rsqrt(var + LN_EPS) * g + b


def _rope_fold(t):
    return t + pltpu.roll(t, QK_ROPE, 1)


def _resident(shape):
    nd = len(shape)
    return pl.BlockSpec(shape, lambda *_: (0,) * nd)


def _in_proj_kernel(x_ref, wssd_ref, wlat_ref, wmisc_ref, tab_ref, dtb_ref, qg_ref, kg_ref,
                    z_ref, xbc_ref, cq_ref, ckv_ref, kpe_ref, kpad_ref, dt_ref):
    xb = x_ref[...].astype(BF16)
    z_ref[...] = _dot(xb, wssd_ref[:, :SSM_WIDTH])
    xbc_ref[...] = _dot(xb, wssd_ref[:, SSM_WIDTH:])
    cq_ref[...] = _rms(_dot(xb, wlat_ref[:, :Q_LORA]), qg_ref[...]).astype(BF16)
    ckv_ref[...] = _rms(_dot(xb, wlat_ref[:, Q_LORA:]), kg_ref[...])
    kr = _rope_fold(_dot(xb, wmisc_ref[:, :LANES]) * tab_ref[...])
    kpe_ref[...] = kr[:, :QK_ROPE]
    lane = lax.broadcasted_iota(jnp.int32, kr.shape, 1)
    kpad_ref[...] = jnp.where(lane < QK_ROPE, kr, 0.0).astype(BF16)
    dt_ref[...] = _softplus(_dot(xb, wmisc_ref[:, LANES:]) + dtb_ref[...])


def _in_proj(x, w1, tab, dtb, qg, kg, tm):
    n = x.shape[0]
    n_tab = tab.shape[0] // tm
    row = lambda w: pl.BlockSpec((tm, w), lambda i: (i, 0))
    return pl.pallas_call(
        _in_proj_kernel,
        grid=(n // tm,),
        in_specs=[row(D_MODEL)] + [_resident(w.shape) for w in w1] + [
                  pl.BlockSpec((tm, LANES), lambda i: (i % n_tab, 0)),
                  _resident(dtb.shape), _resident(qg.shape), _resident(kg.shape)],
        out_specs=[row(SSM_WIDTH), row(CONV_DIM), row(Q_LORA), row(KV_LORA), row(QK_ROPE), row(LANES),
                   row(LANES)],
        out_shape=[jax.ShapeDtypeStruct((n, SSM_WIDTH), F32), jax.ShapeDtypeStruct((n, CONV_DIM), F32),
                   jax.ShapeDtypeStruct((n, Q_LORA), BF16), jax.ShapeDtypeStruct((n, KV_LORA), F32),
                   jax.ShapeDtypeStruct((n, QK_ROPE), F32), jax.ShapeDtypeStruct((n, LANES), BF16),
                   jax.ShapeDtypeStruct((n, LANES), F32)],
        compiler_params=_cparams("arbitrary"),
        name="in_proj",
    )(x, *w1, tab, dtb, qg, kg)


def _gate_and_norm(y, z, nw):
    yg = y * _silu(z)
    parts = []
    for g in range(SSM_GROUPS):
        v = yg[:, g * GROUP_WIDTH:(g + 1) * GROUP_WIDTH]
        parts.append(v * lax.rsqrt(jnp.mean(v * v, axis=-1, keepdims=True) + RMS_EPS))
    return jnp.concatenate(parts, axis=1) * nw


def _ssd_kernel(xbc_ref, z_ref, dt_ref, cw_ref, cb_ref, alog_ref, dsk_ref, nw_ref, e_ref, tril_ref,
                y_ref, hout_ref, ext_ref, ht_ref):
    c = pl.program_id(1)
    L = SSM_CHUNK

    @pl.when(c == 0)
    def _():
        ext_ref[0:SUBLANES, :] = jnp.zeros((SUBLANES, CONV_DIM), F32)
        ht_ref[...] = jnp.zeros_like(ht_ref)

    ext_ref[SUBLANES:SUBLANES + L, :] = xbc_ref[...]
    conv = cb_ref[...] + cw_ref[SSM_CONV - 1:SSM_CONV, :] * xbc_ref[...]
    for k in range(1, SSM_CONV):
        conv = conv + cw_ref[SSM_CONV - 1 - k:SSM_CONV - k, :] * ext_ref[pl.ds(SUBLANES - k, L), :]
    ext_ref[0:SUBLANES, :] = ext_ref[L:L + SUBLANES, :]
    xc = _silu(conv)
    xs = xc[:, :SSM_WIDTH]
    bm = xc[:, SSM_WIDTH:SSM_WIDTH + SSM_GROUPS * SSM_STATE]
    cm = xc[:, SSM_WIDTH + SSM_GROUPS * SSM_STATE:]

    dt = dt_ref[...]
    da = dt * (-jnp.exp(alog_ref[...]))
    acs = _dot_exact_rhs(tril_ref[...], da)
    e = e_ref[...]
    acs_x = _dot_exact_lhs(acs, e)
    dt_x = _dot_exact_lhs(dt, e)
    last_x = acs_x[L - 1:L, :]
    xw = xs * dt_x * jnp.exp(last_x - acs_x)
    exp_acs = jnp.exp(acs_x)
    acs_t = acs.T
    dt_t = dt.T

    row = lax.broadcasted_iota(jnp.int32, (L, L), 0)
    col = lax.broadcasted_iota(jnp.int32, (L, L), 1)
    causal = row >= col
    lower_half = col < SSM_HEAD_DIM

    hprev = ht_ref[...].astype(BF16)
    y_parts = []
    heads_per_group = SSM_HEADS // SSM_GROUPS
    for g in range(SSM_GROUPS):
        gs = slice(g * GROUP_WIDTH, (g + 1) * GROUP_WIDTH)
        bg = bm[:, g * SSM_STATE:(g + 1) * SSM_STATE]
        cg = cm[:, g * SSM_STATE:(g + 1) * SSM_STATE].astype(BF16)
        cb = _dot_nt(cg, bg.astype(BF16))
        y_off = _dot(cg, hprev[:, gs]) * exp_acs[:, gs]
        for jj in range(heads_per_group // 2):
            j = g * (heads_per_group // 2) + jj
            xp = xs[:, j * LANES:(j + 1) * LANES]
            yp = None
            for h, xh in ((2 * j, jnp.where(lower_half, xp, 0.0)), (2 * j + 1, jnp.where(lower_half, 0.0, xp))):
                diff = acs[:, h:h + 1] - acs_t[h:h + 1, :]
                dec = jnp.exp(jnp.where(causal, diff, NEG_BIG))
                m = (cb * dec * dt_t[h:h + 1, :]).astype(BF16)
                t = _dot(m, xh.astype(BF16))
                yp = t if yp is None else yp + t
            y_parts.append(yp + y_off[:, jj * LANES:(jj + 1) * LANES])
        ht_ref[:, gs] = ht_ref[:, gs] * jnp.exp(last_x[:, gs]) + _dot(bg.T.astype(BF16), xw[:, gs].astype(BF16))

    y = jnp.concatenate(y_parts, axis=1) + dsk_ref[...] * xs
    y_ref[...] = _gate_and_norm(y, z_ref[...], nw_ref[...]).astype(BF16)

    @pl.when(c == pl.num_programs(1) - 1)
    def _():
        hout_ref[0] = ht_ref[...].T


def _ssd_prompt(xbc, z, dt, cw, cb, alog, dsk, nw, e, tril, b, t):
    nc = t // SSM_CHUNK
    L = SSM_CHUNK
    row = lambda w: pl.BlockSpec((L, w), lambda bi, c: (bi * nc + c, 0))
    return pl.pallas_call(
        _ssd_kernel,
        grid=(b, nc),
        in_specs=[row(CONV_DIM), row(SSM_WIDTH), row(LANES)] +
                 [_resident(a.shape) for a in (cw, cb, alog, dsk, nw, e, tril)],
        out_specs=[row(SSM_WIDTH), pl.BlockSpec((1, SSM_WIDTH, SSM_STATE), lambda bi, c: (bi, 0, 0))],
        out_shape=[jax.ShapeDtypeStruct((b * t, SSM_WIDTH), BF16),
                   jax.ShapeDtypeStruct((b, SSM_WIDTH, SSM_STATE), F32)],
        scratch_shapes=[pltpu.VMEM((L + SUBLANES, CONV_DIM), F32), pltpu.VMEM((SSM_STATE, SSM_WIDTH), F32)],
        compiler_params=_cparams("arbitrary", "arbitrary"),
        name="ssd_prompt",
    )(xbc, z, dt, cw, cb, alog, dsk, nw, e, tril)


def _ssd_step_kernel(xbc_ref, sc_ref, z_ref, dt_ref, cw_ref, cb_ref, alog_ref, dsk_ref, nw_ref, e_ref, st_ref,
                     y_ref, so_ref, xs_s, b_s, ct_s, xdt_s, da_s, yt_s):
    blk = pl.program_id(0)
    per_step = st_ref.shape[0]

    @pl.when(blk == 0)
    def _():
        conv = cb_ref[...] + cw_ref[SSM_CONV - 1:SSM_CONV, :] * xbc_ref[...]
        for k in range(SSM_CONV - 1):
            conv = conv + cw_ref[k:k + 1, :] * sc_ref[k]
        xc = _silu(conv)
        xs = xc[:, :SSM_WIDTH]
        xs_s[...] = xs
        b_s[...] = xc[:, SSM_WIDTH:SSM_WIDTH + SSM_GROUPS * SSM_STATE]
        ct_s[...] = xc[:, SSM_WIDTH + SSM_GROUPS * SSM_STATE:].T
        dt = dt_ref[...]
        xdt_s[...] = (xs * _dot_exact_lhs(dt, e_ref[...])).T.astype(BF16)
        da_s[...] = jnp.exp(dt * (-jnp.exp(alog_ref[...])))
        yt_s[...] = jnp.zeros_like(yt_s)

    n_samp = xs_s.shape[0]
    row = lax.broadcasted_iota(jnp.int32, (n_samp, SSM_STATE), 0)
    lane = lax.broadcasted_iota(jnp.int32, (SSM_STATE, n_samp), 1)
    heads_per_group = SSM_HEADS // SSM_GROUPS
    for k in range(per_step):
        s = blk * per_step + k
        pick_row = row == s
        decay = jnp.broadcast_to(da_s[pl.ds(s, 1), :], (SSM_STATE, SSM_STATE)).T
        b_row = b_s[pl.ds(s, 1), :]
        for g in range(SSM_GROUPS):
            gs = slice(g * GROUP_WIDTH, (g + 1) * GROUP_WIDTH)
            ns = slice(g * SSM_STATE, (g + 1) * SSM_STATE)
            eb = jnp.where(pick_row, b_row[:, ns], 0.0).astype(BF16)
            upd = _dot(xdt_s[gs, :], eb)
            for hh in range(heads_per_group):
                h = g * heads_per_group + hh
                rs = slice(h * SSM_HEAD_DIM, (h + 1) * SSM_HEAD_DIM)
                so_ref[k, rs, :] = (st_ref[k, rs, :] * decay[h:h + 1, :]
                                    + upd[hh * SSM_HEAD_DIM:(hh + 1) * SSM_HEAD_DIM, :])
            ce = jnp.where(lane == s, ct_s[ns, :], 0.0).astype(BF16)
            yt_s[gs, :] += _dot(so_ref[k, gs, :].astype(BF16), ce)

    @pl.when(blk == pl.num_programs(0) - 1)
    def _():
        y = yt_s[...].T + dsk_ref[...] * xs_s[...]
        y_ref[...] = _gate_and_norm(y, z_ref[...], nw_ref[...]).astype(BF16)


def _ssd_step(xbc, sc, z, dt, cw, cb, alog, dsk, nw, e, state):
    n_s = xbc.shape[0]
    per_step = _tile(n_s, SSD_STEP_SAMPLES)
    st_spec = pl.BlockSpec((per_step, SSM_WIDTH, SSM_STATE), lambda s: (s, 0, 0))
    return pl.pallas_call(
        _ssd_step_kernel,
        grid=(n_s // per_step,),
        in_specs=[_resident(a.shape) for a in (xbc, sc, z, dt, cw, cb, alog, dsk, nw, e)] + [st_spec],
        out_specs=[_resident((n_s, SSM_WIDTH)), st_spec],
        out_shape=[jax.ShapeDtypeStruct((n_s, SSM_WIDTH), BF16),
                   jax.ShapeDtypeStruct((n_s, SSM_WIDTH, SSM_STATE), F32)],
        scratch_shapes=[pltpu.VMEM((n_s, SSM_WIDTH), F32), pltpu.VMEM((n_s, SSM_GROUPS * SSM_STATE), F32),
                        pltpu.VMEM((SSM_GROUPS * SSM_STATE, n_s), F32), pltpu.VMEM((SSM_WIDTH, n_s), BF16),
                        pltpu.VMEM((n_s, LANES), F32), pltpu.VMEM((SSM_WIDTH, n_s), F32)],
        compiler_params=_cparams("arbitrary"),
        name="ssd_step",
    )(xbc, sc, z, dt, cw, cb, alog, dsk, nw, e, state)


def _rope_q_head(qh, tab, lane):
    r = _rope_fold(qh[:, QK_NOPE:] * tab)
    return qh[:, :QK_NOPE], jnp.where(lane < QK_ROPE, r, 0.0)


def _qkv_kernel(cq_ref, ckv_ref, kpad_ref, tab_ref, wq_ref, wuk_ref, wuv_ref, q_ref, k_ref, v_ref):
    q = _dot(cq_ref[...], wq_ref[...])
    ckv = ckv_ref[...].astype(BF16)
    kn = _dot(ckv, wuk_ref[...])
    v = _dot(ckv, wuv_ref[...])
    tab = tab_ref[...]
    kpad = kpad_ref[...]
    lane = lax.broadcasted_iota(jnp.int32, tab.shape, 1)
    for h in range(MLA_HEADS):
        nope, pe = _rope_q_head(q[:, h * QK_PAD:(h + 1) * QK_PAD], tab, lane)
        q_ref[0, h] = (jnp.concatenate([nope, pe], axis=1) * EXP2_SCALE).astype(BF16)
        k_ref[0, h] = jnp.concatenate([kn[:, h * QK_NOPE:(h + 1) * QK_NOPE].astype(BF16), kpad], axis=1)
        vt = v[:, h * V_HEAD_DIM:(h + 1) * V_HEAD_DIM].T.astype(BF16)
        tkv = v_ref.shape[-1]
        extra = lax.broadcasted_iota(jnp.int32, (VT_ROWS - V_HEAD_DIM, tkv), 0)
        ones_row = jnp.where(extra == 0, 1.0, 0.0).astype(BF16)
        for c in range(v_ref.shape[2]):
            v_ref[0, h, c, 0:V_HEAD_DIM, :] = vt[:, c * tkv:(c + 1) * tkv]
            v_ref[0, h, c, V_HEAD_DIM:, :] = ones_row


def _qkv_prompt(cq, ckv, kpad, tab, wq, wuk, wuv, b, t, tm, tkv):
    nt = t // tm
    nc = tm // tkv
    row = lambda w: pl.BlockSpec((tm, w), lambda bi, i: (bi * nt + i, 0))
    head = lambda w: pl.BlockSpec((1, MLA_HEADS, tm, w), lambda bi, i: (bi, 0, i, 0))
    return pl.pallas_call(
        _qkv_kernel,
        grid=(b, nt),
        in_specs=[row(Q_LORA), row(KV_LORA), row(LANES), pl.BlockSpec((tm, LANES), lambda bi, i: (i, 0)),
                  _resident(wq.shape), _resident(wuk.shape), _resident(wuv.shape)],
        out_specs=[head(QK_PAD), head(QK_PAD),
                   pl.BlockSpec((1, MLA_HEADS, nc, VT_ROWS, tkv), lambda bi, i: (bi, 0, i, 0, 0))],
        out_shape=[jax.ShapeDtypeStruct((b, MLA_HEADS, t, QK_PAD), BF16),
                   jax.ShapeDtypeStruct((b, MLA_HEADS, t, QK_PAD), BF16),
                   jax.ShapeDtypeStruct((b, MLA_HEADS, t // tkv, VT_ROWS, tkv), BF16)],
        compiler_params=_cparams("arbitrary", "arbitrary"),
        name="qkv_prompt",
    )(cq, ckv, kpad, tab, wq, wuk, wuv)


def _q_step_kernel(cq_ref, tab_ref, wq_ref, wukt_ref, ql_ref, qp_ref):
    q = _dot(cq_ref[...], wq_ref[...])
    tab = tab_ref[...]
    lane = lax.broadcasted_iota(jnp.int32, tab.shape, 1)
    for h in range(MLA_HEADS):
        nope, pe = _rope_q_head(q[:, h * QK_PAD:(h + 1) * QK_PAD], tab, lane)
        ql_ref[h] = _dot(nope.astype(BF16), wukt_ref[h]).astype(BF16)
        qp_ref[h] = pe.astype(BF16)


def _q_step(cq, tab, wq, wukt):
    n_s = cq.shape[0]
    return pl.pallas_call(
        _q_step_kernel,
        grid=(1,),
        in_specs=[_resident(a.shape) for a in (cq, tab, wq, wukt)],
        out_specs=[_resident((MLA_HEADS, n_s, KV_LORA)), _resident((MLA_HEADS, n_s, LANES))],
        out_shape=[jax.ShapeDtypeStruct((MLA_HEADS, n_s, KV_LORA), BF16),
                   jax.ShapeDtypeStruct((MLA_HEADS, n_s, LANES), BF16)],
        compiler_params=_cparams("arbitrary"),
        name="q_step",
    )(cq, tab, wq, wukt)


def _softmax_update(s, m_ref, l_ref):
    m_prev = m_ref[...]
    m_new = jnp.maximum(m_prev, jnp.max(s, axis=-1, keepdims=True))
    corr = jnp.exp2((m_prev - m_new) * EXP2_SCALE)
    p = jnp.exp2((s - m_new) * EXP2_SCALE)
    l_ref[...] = l_ref[...] * corr + jnp.sum(p, axis=-1, keepdims=True)
    m_ref[...] = m_new
    return p, corr


def _flash_kernel(q_ref, k_ref, vt_ref, o_ref, *scratch, tq, nh):
    m_s, acc_s = scratch[:nh], scratch[nh:]
    qi = pl.program_id(2)
    for h in range(nh):
        m_s[h][...] = jnp.full_like(m_s[h], NEG_BIG)
        acc_s[h][...] = jnp.zeros_like(acc_s[h])

    def step(j, masked):
        start = pl.multiple_of(j * tq, tq)
        sts = [_dot_nt(k_ref[0, h, pl.ds(start, tq), :], q_ref[0, h]) for h in range(nh)]
        ps, corrs = [], []
        for h in range(nh):
            st = sts[h]
            if masked:
                key = lax.broadcasted_iota(jnp.int32, st.shape, 0)
                qry = lax.broadcasted_iota(jnp.int32, st.shape, 1)
                st = jnp.where(key <= qry, st, NEG_BIG)
            m_prev = m_s[h][...]
            m_new = jnp.maximum(m_prev, jnp.max(st, axis=0, keepdims=True))
            corr = jnp.exp2(m_prev - m_new)
            m_s[h][...] = m_new
            ps.append(jnp.exp2(st - m_new).astype(BF16))
            corrs.append(corr)
        for h in range(nh):
            acc_s[h][...] = acc_s[h][...] * corrs[h] + _dot(vt_ref[0, h, j], ps[h])

    def body(j, carry):
        step(j, False)
        return carry

    lax.fori_loop(0, qi, body, 0)
    step(qi, True)
    for h in range(nh):
        o = acc_s[h][0:V_HEAD_DIM, :] / acc_s[h][V_HEAD_DIM:V_HEAD_DIM + 1, :]
        o_ref[0, :, h * V_HEAD_DIM:(h + 1) * V_HEAD_DIM] = o.T.astype(BF16)


def _flash_prompt(q, k, vt, tq):
    b, h, t, _ = q.shape
    nh = FLASH_HEADS
    return pl.pallas_call(
        functools.partial(_flash_kernel, tq=tq, nh=nh),
        grid=(b, h // nh, t // tq),
        in_specs=[pl.BlockSpec((1, nh, tq, QK_PAD), lambda bi, hi, qi: (bi, hi, qi, 0)),
                  pl.BlockSpec((1, nh, t, QK_PAD), lambda bi, hi, qi: (bi, hi, 0, 0)),
                  pl.BlockSpec((1, nh, t // tq, VT_ROWS, tq), lambda bi, hi, qi: (bi, hi, 0, 0, 0))],
        out_specs=pl.BlockSpec((1, tq, nh * V_HEAD_DIM), lambda bi, hi, qi: (bi, qi, hi)),
        out_shape=jax.ShapeDtypeStruct((b, t, h * V_HEAD_DIM), BF16),
        scratch_shapes=([pltpu.VMEM((1, tq), F32)] * nh + [pltpu.VMEM((VT_ROWS, tq), F32)] * nh),
        compiler_params=_cparams("arbitrary", "arbitrary", "arbitrary"),
        name="flash_prompt",
    )(q, k, vt)


def _decode_kernel(pt_ref, ql_ref, qp_ref, cn_ref, kn_ref, ckv_hbm, kpe_hbm, o_ref,
                   kc_buf, kp_buf, sem, m_s, l_s, acc_s, kcb0, kcb1, sc0, sc1, *, n_pg, n_groups, page):
    s = pl.program_id(0)
    n_slots = kc_buf.shape[0]
    ahead = n_slots - 1
    total = pl.num_programs(0) * n_groups

    def copies(samp, grp, slot):
        out = []
        for i in range(n_pg):
            pg = pt_ref[samp, grp * n_pg + i]
            out.append(pltpu.make_async_copy(ckv_hbm.at[pg], kc_buf.at[slot, pl.ds(i * page, page), :],
                                             sem.at[0, slot]))
            out.append(pltpu.make_async_copy(kpe_hbm.at[pg], kp_buf.at[slot, i], sem.at[1, slot]))
        return out

    def start(samp, grp, slot):
        for c in copies(samp, grp, slot):
            c.start()

    @pl.when(s == 0)
    def _():
        for t in range(ahead):
            start(t // n_groups, t % n_groups, t % n_slots)

    ql = ql_ref[0]
    qp = qp_ref[0][:, :QK_ROPE]
    cn = cn_ref[...].astype(BF16).astype(F32)
    kn = kn_ref[...].astype(BF16).astype(F32)
    m_s[...] = (jnp.sum(ql.astype(F32) * cn, axis=-1, keepdims=True)
                + jnp.sum(qp_ref[0].astype(F32) * kn, axis=-1, keepdims=True))
    l_s[...] = jnp.ones_like(l_s)
    acc_s[...] = jnp.broadcast_to(cn, acc_s.shape)

    def fetch_ahead(g):
        t = s * n_groups + g
        g_next = g + ahead
        wraps = g_next >= n_groups

        @pl.when(t + ahead < total)
        def _():
            start(jnp.where(wraps, s + 1, s), jnp.where(wraps, g_next - n_groups, g_next),
                  lax.rem(t + ahead, n_slots))

    def load_scores(g, kcb, sc):
        slot = lax.rem(s * n_groups + g, n_slots)
        for c in copies(s, g, slot):
            c.wait()
        kc = kc_buf[slot].astype(BF16)
        kcb[...] = kc
        s_pe = [_dot(qp, kp_buf[slot, i].astype(BF16)) for i in range(n_pg)]
        sc[...] = _dot_nt(ql, kc) + jnp.concatenate(s_pe, axis=1)

    def accumulate(kcb, sc):
        p, corr = _softmax_update(sc[...], m_s, l_s)
        acc_s[...] = acc_s[...] * corr + _dot(p.astype(BF16), kcb[...])

    fetch_ahead(0)
    load_scores(0, kcb0, sc0)

    def pair(k, carry):
        g = 2 * k
        fetch_ahead(g + 1)
        load_scores(g + 1, kcb1, sc1)
        accumulate(kcb0, sc0)
        fetch_ahead(g + 2)
        load_scores(g + 2, kcb0, sc0)
        accumulate(kcb1, sc1)
        return carry

    lax.fori_loop(0, n_groups // 2 - 1, pair, 0)
    fetch_ahead(n_groups - 1)
    load_scores(n_groups - 1, kcb1, sc1)
    accumulate(kcb0, sc0)
    accumulate(kcb1, sc1)
    o_ref[0] = acc_s[...] / l_s[...]


def _decode_attention(page_table, ql, qp, ckv_new, kpe_new, ckv_pool, kpe_pool_t):
    n_s, n_pages = page_table.shape
    page = ckv_pool.shape[1]
    n_pg = min(DECODE_PAGES, n_pages // 2)
    n_slots = DECODE_SLOTS
    assert n_pages % (2 * n_pg) == 0 and n_slots - 1 <= n_pages // n_pg
    grid_spec = pltpu.PrefetchScalarGridSpec(
        num_scalar_prefetch=1,
        grid=(n_s,),
        in_specs=[pl.BlockSpec((1, HEAD_PAD, KV_LORA), lambda s, pt: (s, 0, 0)),
                  pl.BlockSpec((1, HEAD_PAD, LANES), lambda s, pt: (s, 0, 0)),
                  pl.BlockSpec((None, 1, KV_LORA), lambda s, pt: (s, 0, 0)),
                  pl.BlockSpec((None, 1, LANES), lambda s, pt: (s, 0, 0)),
                  pl.BlockSpec(memory_space=pl.ANY), pl.BlockSpec(memory_space=pl.ANY)],
        out_specs=pl.BlockSpec((1, HEAD_PAD, KV_LORA), lambda s, pt: (s, 0, 0)),
        scratch_shapes=[pltpu.VMEM((n_slots, n_pg * page, KV_LORA), F32),
                        pltpu.VMEM((n_slots, n_pg, QK_ROPE, page), F32),
                        pltpu.SemaphoreType.DMA((2, n_slots)),
                        pltpu.VMEM((HEAD_PAD, 1), F32), pltpu.VMEM((HEAD_PAD, 1), F32),
                        pltpu.VMEM((HEAD_PAD, KV_LORA), F32),
                        pltpu.VMEM((n_pg * page, KV_LORA), BF16), pltpu.VMEM((n_pg * page, KV_LORA), BF16),
                        pltpu.VMEM((HEAD_PAD, n_pg * page), F32), pltpu.VMEM((HEAD_PAD, n_pg * page), F32)],
    )
    return pl.pallas_call(
        functools.partial(_decode_kernel, n_pg=n_pg, n_groups=n_pages // n_pg, page=page),
        grid_spec=grid_spec,
        out_shape=jax.ShapeDtypeStruct((n_s, HEAD_PAD, KV_LORA), F32),
        compiler_params=_cparams("arbitrary"),
        name="decode_attention",
    )(page_table, ql, qp, ckv_new, kpe_new, ckv_pool, kpe_pool_t)


def _uv_kernel(o_ref, wuv_ref, y_ref):
    for h in range(MLA_HEADS):
        y_ref[:, h * V_HEAD_DIM:(h + 1) * V_HEAD_DIM] = _dot(o_ref[h], wuv_ref[h]).astype(BF16)


def _uv_step(o, wuv):
    n_s = o.shape[1]
    return pl.pallas_call(
        _uv_kernel,
        grid=(1,),
        in_specs=[_resident(o.shape), _resident(wuv.shape)],
        out_specs=_resident((n_s, MLA_HEADS * V_HEAD_DIM)),
        out_shape=jax.ShapeDtypeStruct((n_s, MLA_HEADS * V_HEAD_DIM), BF16),
        compiler_params=_cparams("arbitrary"),
        name="uv_step",
    )(o, wuv)


def _wo_ln_kernel(ys_ref, ym_ref, x_ref, wo_ref, g_ref, b_ref, h_ref, hb_ref):
    rows = x_ref.shape[0] // WO_PARTS
    parts = [slice(r * rows, (r + 1) * rows) for r in range(WO_PARTS)]
    mixes = [_dot(ys_ref[rs, :], wo_ref[0:SSM_WIDTH, :]) + _dot(ym_ref[rs, :], wo_ref[SSM_WIDTH:, :])
             for rs in parts]
    for rs, mix in zip(parts, mixes):
        h = _layer_norm(ALPHA * x_ref[rs, :] + mix, g_ref[...], b_ref[...])
        h_ref[rs, :] = h
        hb_ref[rs, :] = h.astype(BF16)


def _wo_ln(ys, ym, x, wo, g, b, tm):
    n = x.shape[0]
    row = lambda w: pl.BlockSpec((tm, w), lambda i: (i, 0))
    return pl.pallas_call(
        _wo_ln_kernel,
        grid=(n // tm,),
        in_specs=[row(SSM_WIDTH), row(SSM_WIDTH), row(D_MODEL), _resident(wo.shape), _resident(g.shape),
                  _resident(b.shape)],
        out_specs=[row(D_MODEL), row(D_MODEL)],
        out_shape=[jax.ShapeDtypeStruct((n, D_MODEL), F32), jax.ShapeDtypeStruct((n, D_MODEL), BF16)],
        compiler_params=_cparams("arbitrary"),
        name="wo_ln",
    )(ys, ym, x, wo, g, b)


def _ffn_begin(j, acc_s):
    @pl.when(j == 0)
    def _():
        acc_s[...] = jnp.zeros_like(acc_s)


def _ffn_gate_up(hb_ref, wg_ref, wu_ref):
    hb = hb_ref[...]
    w = wg_ref.shape[1] // FFN_PARTS
    parts = [slice(c * w, (c + 1) * w) for c in range(FFN_PARTS)]
    return [(cs, _dot(hb, wg_ref[:, cs]), _dot(hb, wu_ref[:, cs])) for cs in parts]


def _ffn_end(j, h_ref, g2_ref, b2_ref, y_ref, acc_s):
    @pl.when(j == pl.num_programs(1) - 1)
    def _():
        y_ref[...] = _layer_norm(ALPHA * h_ref[...] + acc_s[...], g2_ref[...], b2_ref[...])


def _ffn_prompt_kernel(h_ref, hb_ref, wg_ref, wu_ref, wd_ref, cw_ref, cb_ref, g2_ref, b2_ref, y_ref, gl_ref,
                       acc_s, ext_s, carry_s, *, blocks_per_seq):
    i = pl.program_id(0)
    j = pl.program_id(1)
    tm = h_ref.shape[0]
    _ffn_begin(j, acc_s)

    @pl.when(i % blocks_per_seq == 0)
    def _():
        carry_s[j] = jnp.zeros(carry_s.shape[1:], F32)

    for cs, g, u in _ffn_gate_up(hb_ref, wg_ref, wu_ref):
        ext_s[0:SUBLANES, cs] = carry_s[j, :, cs]
        ext_s[SUBLANES:, cs] = g
        gc = cb_ref[:, cs] + cw_ref[FFN_CONV - 1:FFN_CONV, cs] * g
        for k in range(1, FFN_CONV):
            gc = gc + cw_ref[FFN_CONV - 1 - k:FFN_CONV - k, cs] * ext_s[pl.ds(SUBLANES - k, tm), cs]
        tail = ext_s[tm:tm + SUBLANES, cs]
        carry_s[j, :, cs] = tail
        gl_ref[0, :, cs] = tail
        acc_s[...] += _dot((_silu(gc) * u).astype(BF16), wd_ref[cs, :])
    _ffn_end(j, h_ref, g2_ref, b2_ref, y_ref, acc_s)


def _ffn_step_kernel(h_ref, hb_ref, wg_ref, wu_ref, wd_ref, cw_ref, cb_ref, p2_ref, p1_ref, g2_ref, b2_ref,
                     y_ref, gout_ref, acc_s):
    j = pl.program_id(1)
    _ffn_begin(j, acc_s)
    for cs, g, u in _ffn_gate_up(hb_ref, wg_ref, wu_ref):
        gout_ref[:, cs] = g
        gc = (cb_ref[:, cs] + cw_ref[0:1, cs] * p2_ref[:, cs] + cw_ref[1:2, cs] * p1_ref[:, cs]
              + cw_ref[2:3, cs] * g)
        acc_s[...] += _dot((_silu(gc) * u).astype(BF16), wd_ref[cs, :])
    _ffn_end(j, h_ref, g2_ref, b2_ref, y_ref, acc_s)


def _ffn_specs(tm, tf):
    return [pl.BlockSpec((tm, D_MODEL), lambda i, j: (i, 0)),
            pl.BlockSpec((tm, D_MODEL), lambda i, j: (i, 0)),
            pl.BlockSpec((D_MODEL, tf), lambda i, j: (0, j)),
            pl.BlockSpec((D_MODEL, tf), lambda i, j: (0, j + D_FF // tf)),
            pl.BlockSpec((tf, D_MODEL), lambda i, j: (j, 0)),
            pl.BlockSpec((FFN_CONV, tf), lambda i, j: (0, j)),
            pl.BlockSpec((1, tf), lambda i, j: (0, j))]


def _ffn_prompt(h, hb, w_in, w_down, cw, cb, g2, b2, b, t, tm, tf):
    n = h.shape[0]
    nj = D_FF // tf
    bps = t // tm
    vec = pl.BlockSpec((1, D_MODEL), lambda i, j: (0, 0))
    return pl.pallas_call(
        functools.partial(_ffn_prompt_kernel, blocks_per_seq=bps),
        grid=(n // tm, nj),
        in_specs=_ffn_specs(tm, tf) + [vec, vec],
        out_specs=[pl.BlockSpec((tm, D_MODEL), lambda i, j: (i, 0)),
                   pl.BlockSpec((1, SUBLANES, tf), lambda i, j: (i, 0, j))],
        out_shape=[jax.ShapeDtypeStruct((n, D_MODEL), F32), jax.ShapeDtypeStruct((n // tm, SUBLANES, D_FF), F32)],
        scratch_shapes=[pltpu.VMEM((tm, D_MODEL), F32),
                        pltpu.VMEM((tm + SUBLANES, tf), F32), pltpu.VMEM((nj, SUBLANES, tf), F32)],
        compiler_params=_cparams("arbitrary", "arbitrary"),
        name="ffn_prompt",
    )(h, hb, w_in, w_in, w_down, cw, cb, g2, b2)


def _ffn_step(h, hb, w_in, w_down, cw, cb, p2, p1, g2, b2, tf):
    n = h.shape[0]
    nj = D_FF // tf
    vec = pl.BlockSpec((1, D_MODEL), lambda i, j: (0, 0))
    col = pl.BlockSpec((n, tf), lambda i, j: (0, j))
    return pl.pallas_call(
        _ffn_step_kernel,
        grid=(1, nj),
        in_specs=_ffn_specs(n, tf) + [col, col, vec, vec],
        out_specs=[pl.BlockSpec((n, D_MODEL), lambda i, j: (0, 0)), col],
        out_shape=[jax.ShapeDtypeStruct((n, D_MODEL), F32), jax.ShapeDtypeStruct((n, D_FF), F32)],
        scratch_shapes=[pltpu.VMEM((n, D_MODEL), F32)],
        compiler_params=_cparams("arbitrary", "arbitrary"),
        name="ffn_step",
    )(h, hb, w_in, w_in, w_down, cw, cb, p2, p1, g2, b2)


def _rope_table(pos):
    inv_freq = ROPE_THETA ** (-jnp.arange(0, QK_ROPE, 2, dtype=F32) / QK_ROPE)
    ang = pos.astype(F32)[:, None] * inv_freq[None, :]
    c, s = jnp.cos(ang), jnp.sin(ang)
    return jnp.concatenate([c, c, -s, s], axis=1)


def _swap_halves(w):
    half = w.shape[-1] // 2
    return jnp.concatenate([w[..., half:], w[..., :half]], axis=-1)


def _pad_lanes(v):
    return jnp.pad(v.reshape(1, -1).astype(F32), ((0, 0), (0, LANES - v.shape[-1])))


def _tile(n, cap):
    t = min(n, cap)
    assert n % t == 0
    return t


def kernel(x_prompt, x_sample, cache_ckv, cache_kpe, page_table, state_ssm, state_conv, state_ffn_conv,
           w_in, conv_w, conv_b, dt_bias, a_log, d_skip, ssm_norm_w, q_norm_w, kv_norm_w,
           w_uq, w_uk, w_uv, w_o, ln1_g, ln1_b, w_ffn_in, ffn_conv_w, ffn_conv_b, w_ffn_down, ln2_g, ln2_b):
    assert w_in.shape[0] == DEPTH == 1 and x_sample.shape[1] == 1
    b, t, _ = x_prompt.shape
    n_s = x_sample.shape[0]
    n_pages = page_table.shape[1]
    past_len = n_pages * cache_ckv.shape[2]
    assert t % SSM_CHUNK == 0

    wi = w_in[0]
    w_kpe = wi[:, OFF_CKV:]
    w_misc = jnp.concatenate([w_kpe, _swap_halves(w_kpe), wi[:, OFF_XBC:OFF_DT],
                              jnp.zeros((D_MODEL, LANES - SSM_HEADS), F32)], axis=1)
    w1 = (wi[:, :OFF_XBC].astype(BF16), wi[:, OFF_DT:OFF_CKV].astype(BF16), w_misc.astype(BF16))
    uq = w_uq[0]
    uq_pe = uq[:, :, QK_NOPE:]
    wq = jnp.concatenate([uq[:, :, :QK_NOPE], uq_pe, _swap_halves(uq_pe)], axis=-1)
    wq = wq.reshape(Q_LORA, MLA_HEADS * QK_PAD).astype(BF16)
    wuk = w_uk[0].reshape(KV_LORA, MLA_HEADS * QK_NOPE).astype(BF16)
    wuv = w_uv[0].reshape(KV_LORA, MLA_HEADS * V_HEAD_DIM).astype(BF16)
    wukt = jnp.transpose(w_uk[0], (1, 2, 0)).astype(BF16)
    wuv_h = jnp.transpose(w_uv[0], (1, 0, 2)).astype(BF16)
    wo = w_o[0].astype(BF16)
    wf_in = w_ffn_in[0].astype(BF16)
    wf_down = w_ffn_down[0].astype(BF16)
    row = lambda v: v.reshape(1, -1).astype(F32)
    dtb, alog = _pad_lanes(dt_bias[0]), _pad_lanes(a_log[0])
    dsk = row(jnp.repeat(d_skip[0], SSM_HEAD_DIM))
    nw, qg, kg = row(ssm_norm_w[0]), row(q_norm_w[0]), row(kv_norm_w[0])
    cw, cb = conv_w[0], row(conv_b[0])
    fcw, fcb = ffn_conv_w[0], row(ffn_conv_b[0])
    g1, b1, g2, b2 = row(ln1_g[0]), row(ln1_b[0]), row(ln2_g[0]), row(ln2_b[0])
    expand = (jnp.arange(SSM_WIDTH)[None, :] // SSM_HEAD_DIM == jnp.arange(LANES)[:, None]).astype(BF16)
    tril = (jnp.arange(SSM_CHUNK)[:, None] >= jnp.arange(SSM_CHUNK)[None, :]).astype(BF16)
    tab_p = _rope_table(jnp.arange(t))
    tab_s = _rope_table(jnp.full((n_s,), past_len))

    tm = _tile(t, ROW_TILE)
    xp = x_prompt.reshape(b * t, D_MODEL)
    z, xbc, cq, ckv, kpe, kpad, dt = _in_proj(xp, w1, tab_p, dtb, qg, kg, tm)
    y_ssd, h_fin = _ssd_prompt(xbc, z, dt, cw, cb, alog, dsk, nw, expand, tril, b, t)
    tq = _tile(tm, FLASH_TILE)
    q, k, vt = _qkv_prompt(cq, ckv, kpad, tab_p, wq, wuk, wuv, b, t, tm, tq)
    y_mla = _flash_prompt(q, k, vt, tq).reshape(b * t, MLA_HEADS * V_HEAD_DIM)
    h1, h1b = _wo_ln(y_ssd, y_mla, xp, wo, g1, b1, tm)
    tf = FF_TILE
    tm_ffn = _tile(t, FFN_ROW_TILE)
    y_p, g_last = _ffn_prompt(h1, h1b, wf_in, wf_down, fcw, fcb, g2, b2, b, t, tm_ffn, tf)

    xs_ = x_sample.reshape(n_s, D_MODEL)
    z_s, xbc_s, cq_s, ckv_s, kpe_s, kpad_s, dt_s = _in_proj(xs_, w1, tab_s, dtb, qg, kg, n_s)
    sc = jnp.transpose(state_conv[0], (1, 0, 2))
    y_ssd_s, st_new = _ssd_step(xbc_s, sc, z_s, dt_s, cw, cb, alog, dsk, nw, expand,
                                state_ssm[0].reshape(n_s, SSM_WIDTH, SSM_STATE))
    ql, qp = _q_step(cq_s, tab_s, wq, wukt)
    pad_heads = lambda a: jnp.pad(jnp.transpose(a, (1, 0, 2)), ((0, 0), (0, HEAD_PAD - MLA_HEADS), (0, 0)))
    o_lat = _decode_attention(page_table, pad_heads(ql), pad_heads(qp), ckv_s.reshape(n_s, 1, KV_LORA),
                              kpad_s.astype(F32).reshape(n_s, 1, LANES), cache_ckv[0],
                              jnp.swapaxes(cache_kpe[0], 1, 2))
    y_mla_s = _uv_step(jnp.transpose(o_lat[:, :MLA_HEADS], (1, 0, 2)).astype(BF16), wuv_h)
    h1_s, h1b_s = _wo_ln(y_ssd_s, y_mla_s, xs_, wo, g1, b1, n_s)
    fbuf = state_ffn_conv[0]
    y_s, g_s = _ffn_step(h1_s, h1b_s, wf_in, wf_down, fcw, fcb, fbuf[:, 0], fbuf[:, 1], g2, b2, tf)

    lead = lambda a: a[None]
    return (y_p.reshape(b, t, D_MODEL),
            y_s.reshape(n_s, 1, D_MODEL),
            lead(ckv.reshape(b, t, KV_LORA)),
            lead(kpe.reshape(b, t, QK_ROPE)),
            lead(h_fin.reshape(b, SSM_HEADS, SSM_HEAD_DIM, SSM_STATE)),
            lead(xbc.reshape(b, t, CONV_DIM)[:, t - (SSM_CONV - 1):]),
            lead(g_last.reshape(b, t // tm_ffn, SUBLANES, D_FF)[:, -1, SUBLANES - (FFN_CONV - 1):]),
            lead(ckv_s.reshape(n_s, 1, KV_LORA)),
            lead(kpe_s.reshape(n_s, 1, QK_ROPE)),
            lead(st_new.reshape(n_s, SSM_HEADS, SSM_HEAD_DIM, SSM_STATE)),
            lead(jnp.concatenate([state_conv[0][:, 1:], xbc_s[:, None]], axis=1)),
            lead(jnp.concatenate([fbuf[:, 1:], g_s[:, None]], axis=1)))
```
